```python
import math
import jax, jax.numpy as jnp
from jax import lax
import numpy as np

D_MODEL = 1024
BATCH = 8
SEQ = 2048
DEPTH = 4
DEC_BATCH = 128
DEC_SEQ = 1
PAST_LEN = 16384
PAGE_SIZE = 128

N_BRANCH = 4
BR_WIDTH = D_MODEL // N_BRANCH
HEAD_DIM = 64
N_HEADS = BR_WIDTH // HEAD_DIM
RW_H = N_HEADS
RET_H = N_HEADS
GDN_H = N_HEADS
HG_H = N_HEADS
RW_LORA_W = D_MODEL // 16
RW_LORA_A = D_MODEL // 16
RW_LORA_G = D_MODEL // 8
RW_COLS = 3 * BR_WIDTH + RW_LORA_W + RW_LORA_A + RW_LORA_G
RW_GN_EPS = 64e-5
RET_COLS = 4 * BR_WIDTH
ROPE_BASE = 10000.0
GDN_CONV = 4
GDN_CONV_CH = 3 * BR_WIDTH
GDN_COLS = 4 * BR_WIDTH + 2 * GDN_H
HG_COLS = 4 * BR_WIDTH
GATE_COLS = N_BRANCH * D_MODEL
OFF_RET = RW_COLS
OFF_GDN = OFF_RET + RET_COLS
OFF_HG = OFF_GDN + GDN_COLS
OFF_GATE = OFF_HG + HG_COLS
IN_COLS = OFF_GATE + GATE_COLS
D_FF = ((8 * D_MODEL + 3 * 256 - 1) // (3 * 256)) * 256
CHUNK = 64
NORM_EPS = 1e-6

kernel_name = 'hybrid_rwkv7_retnet_gdn_hgrn2_step'


def _rmsnorm(x, w):
    xf = x.astype(jnp.float32)
    y = xf * lax.rsqrt(jnp.mean(xf * xf, -1, keepdims=True) + NORM_EPS)
    return (y * w.astype(jnp.float32)).astype(x.dtype)


def _rms_last(t):
    return t * lax.rsqrt(jnp.mean(t * t, -1, keepdims=True) + NORM_EPS)


def _l2norm(t):
    return t * lax.rsqrt(jnp.sum(t * t, -1, keepdims=True) + 1e-6)


def _heads(t, h):
    return t.reshape(t.shape[:-1] + (h, t.shape[-1] // h))


def _rope(t, pos):
    half = t.shape[-1] // 2
    freqs = ROPE_BASE ** (-jnp.arange(half, dtype=jnp.float32) / half)
    ang = pos.astype(jnp.float32)[:, None] * freqs
    cos, sin = jnp.cos(ang)[:, None, :], jnp.sin(ang)[:, None, :]
    t1, t2 = t[..., :half], t[..., half:]
    return jnp.concatenate([t1 * cos - t2 * sin, t1 * sin + t2 * cos], -1)


def _chunk_len(T):
    return CHUNK if T % CHUNK == 0 else T


def _to_chunks(t, c):
    B, T, H = t.shape[:3]
    t = t.reshape((B, T // c, c, H) + t.shape[3:])
    return jnp.moveaxis(t, (1, 3), (0, 2))


def _from_chunks(t):
    n, B, H, c = t.shape[:4]
    t = jnp.moveaxis(t, (0, 2), (1, 3))
    return t.reshape((B, n * c, H) + t.shape[4:])


def _masked_exp(mask, d):
    return jnp.where(mask, jnp.exp(jnp.where(mask, d, 0.0)), 0.0)


def _rwkv7_recurrence(r, w, k, v, a, b, S0):
    xs = tuple(jnp.moveaxis(t, 1, 0) for t in (r, w, k, v, a, b))

    def step(S, inp):
        rt, wt, kt, vt, at, bt = inp
        sa = jnp.einsum('bhk,bhkv->bhv', at, S)
        S = wt[..., None] * S + bt[..., None] * sa[:, :, None, :] + kt[..., None] * vt[:, :, None, :]
        return S, jnp.einsum('bhk,bhkv->bhv', rt, S)

    S, o = lax.scan(step, S0, xs)
    return jnp.moveaxis(o, 0, 1), S


def _retention_chunked(q, k, v, log_gamma, S0):
    c = _chunk_len(q.shape[1])
    qc, kc, vc = (_to_chunks(t, c) for t in (q, k, v))
    idx = jnp.arange(c, dtype=jnp.float32)
    lg = log_gamma[:, None]
    diff = idx[:, None] - idx[None, :]
    dmat = jnp.where(diff >= 0, jnp.exp(lg[:, :, None] * jnp.maximum(diff, 0.0)), 0.0)
    q_in = jnp.exp(lg * (idx + 1.0))[..., None]
    k_out = jnp.exp(lg * (c - 1.0 - idx))[..., None]
    c_dec = jnp.exp(lg * c)[..., None]

    def step(S, inp):
        qi, ki, vi = inp
        s = jnp.einsum('bhid,bhjd->bhij', qi, ki) * dmat
        o = jnp.einsum('bhij,bhjv->bhiv', s, vi) + jnp.einsum('bhid,bhdv->bhiv', qi * q_in, S)
        S = S * c_dec + jnp.einsum('bhjd,bhjv->bhdv', ki * k_out, vi)
        return S, o

    S, o = lax.scan(step, S0, (qc, kc, vc))
    return _from_chunks(o), S


def _gated_delta_chunked(q, k, v, g, beta, S0):
    c = _chunk_len(q.shape[1])
    dv = v.shape[-1]
    qc, kc, vc = (_to_chunks(t, c) for t in (q, k, v))
    gc, bc = _to_chunks(g, c), _to_chunks(beta, c)
    G = jnp.cumsum(gc, axis=-1)
    incl = jnp.tril(jnp.ones((c, c), dtype=bool))
    strict = jnp.tril(jnp.ones((c, c), dtype=bool), -1)
    gam = _masked_exp(incl, G[..., :, None] - G[..., None, :])
    L = jnp.where(strict, bc[..., :, None] * gam * jnp.einsum('nbhid,nbhjd->nbhij', kc, kc), 0.0)
    A = L + jnp.eye(c, dtype=L.dtype)
    rhs = jnp.concatenate([vc * bc[..., None], kc * (bc * jnp.exp(G))[..., None]], axis=-1)
    sol = lax.linalg.triangular_solve(A, rhs, left_side=True, lower=True, unit_diagonal=True)
    u_v, w_k = sol[..., :dv], sol[..., dv:]
    qk = jnp.einsum('nbhid,nbhjd->nbhij', qc, kc) * gam
    q_in = qc * jnp.exp(G)[..., None]
    k_out = kc * jnp.exp(G[..., -1:] - G)[..., None]
    c_dec = jnp.exp(G[..., -1])

    def step(S, inp):
        u_i, w_i, qk_i, qin_i, kout_i, cd_i = inp
        u = u_i - jnp.einsum('bhik,bhkv->bhiv', w_i, S)
        o = jnp.einsum('bhik,bhkv->bhiv', qin_i, S) + jnp.einsum('bhij,bhjv->bhiv', qk_i, u)
        S = S * cd_i[..., None, None] + jnp.einsum('bhjk,bhjv->bhkv', kout_i, u)
        return S, o

    S, o = lax.scan(step, S0, (u_v, w_k, qk, q_in, k_out, c_dec))
    return _from_chunks(o), S


def _gla_chunked(q, k, v, log_f, S0):
    c = _chunk_len(q.shape[1])
    qc, kc, vc, fc = (_to_chunks(t, c) for t in (q, k, v, log_f))
    G = jnp.cumsum(fc, axis=-2)
    incl = jnp.tril(jnp.ones((c, c), dtype=bool))[:, :, None]

    def step(S, inp):
        qi, ki, vi, Gi = inp
        dec = _masked_exp(incl, Gi[:, :, :, None, :] - Gi[:, :, None, :, :])
        s = jnp.einsum('bhijd,bhjd->bhij', qi[:, :, :, None, :] * dec, ki)
        o = jnp.einsum('bhij,bhjv->bhiv', s, vi) + jnp.einsum('bhid,bhdv->bhiv', qi * jnp.exp(Gi), S)
        g_end = Gi[:, :, -1:, :]
        S = S * jnp.exp(g_end)[:, :, 0, :, None] + jnp.einsum('bhjd,bhjv->bhdv', ki * jnp.exp(g_end - Gi), vi)
        return S, o

    S, o = lax.scan(step, S0, (qc, kc, vc, G))
    return _from_chunks(o), S


def _mixer(h, st, pos, lb, p):
    st_rw, st_sh, st_ret, st_gdn, st_cv, st_hg = st
    B, T, _ = h.shape
    f32 = jnp.float32
    proj = (h @ p['w_in']).astype(f32)
    o1, o2, o3 = BR_WIDTH, 2 * BR_WIDTH, 3 * BR_WIDTH

    P = proj[..., :RW_COLS]
    prev = jnp.concatenate([st_sh.astype(f32), P[:, :-1]], axis=1)
    Pm = P + p['rw_mu'] * (prev - P)
    o4, o5 = o3 + RW_LORA_W, o3 + RW_LORA_W + RW_LORA_A
    r, kx, vx = Pm[..., :o1], Pm[..., o1:o2], Pm[..., o2:o3]
    wl, al, gl = Pm[..., o3:o4], Pm[..., o4:o5], Pm[..., o5:]
    w_log = -jax.nn.softplus(-(p['rw_w0'] + jnp.tanh(wl) @ p['rw_w2'])) - 0.5
    decay = jnp.exp(-jnp.exp(w_log))
    a = jax.nn.sigmoid(p['rw_a0'] + al @ p['rw_a2'])
    g_rw = jax.nn.sigmoid(gl) @ p['rw_g2']
    kk = _l2norm(_heads(kx * p['rw_k_k'], RW_H))
    k_rw = kx * (1.0 + (a - 1.0) * p['rw_k_a'])
    rh, kh, vh, ah = (_heads(t, RW_H) for t in (r, k_rw, vx, a))
    o, s_rw = _rwkv7_recurrence(rh, _heads(decay, RW_H), kh, vh, -kk, kk * ah, st_rw.astype(f32))
    mean = jnp.mean(o, -1, keepdims=True)
    var = jnp.mean(jnp.square(o - mean), -1, keepdims=True)
    o = ((o - mean) * lax.rsqrt(var + RW_GN_EPS)).reshape(B, T, BR_WIDTH) * p['rw_ln_w'] + p['rw_ln_b']
    bonus = jnp.sum(rh * kh * p['rw_r_k'], -1, keepdims=True) * vh
    y_rw = (o + bonus.reshape(B, T, BR_WIDTH)) * g_rw

    Pr = proj[..., OFF_RET:OFF_GDN]
    q = _rope(_heads(Pr[..., :o1], RET_H), pos)
    k = _rope(_heads(Pr[..., o1:o2], RET_H), pos) * HEAD_DIM ** -0.5
    v = _heads(Pr[..., o2:o3], RET_H)
    log_gamma = jnp.log1p(-jnp.exp2(-5.0 - jnp.arange(RET_H, dtype=f32)))
    o, s_ret = _retention_chunked(q, k, v, log_gamma, st_ret.astype(f32))
    y_ret = _rms_last(o).reshape(B, T, BR_WIDTH) * jax.nn.silu(Pr[..., o3:])

    Pg = proj[..., OFF_GDN:OFF_HG]
    xcat = jnp.concatenate([st_cv.astype(f32), Pg[..., :o3]], axis=1)
    conv = xcat[:, :T] * p['gdn_conv_w'][0]
    for j in range(1, GDN_CONV):
        conv = conv + xcat[:, j:j + T] * p['gdn_conv_w'][j]
    conv = jax.nn.silu(conv)
    new_cv = xcat[:, -(GDN_CONV - 1):]
    q = _l2norm(_heads(conv[..., :o1], GDN_H)) * HEAD_DIM ** -0.5
    k = _l2norm(_heads(conv[..., o1:o2], GDN_H))
    v = _heads(conv[..., o2:o3], GDN_H)
    a_off = 4 * BR_WIDTH
    g_log = -jnp.exp(p['gdn_a_log']) * jax.nn.softplus(Pg[..., a_off:a_off + GDN_H] + p['gdn_dt_bias'])
    beta = jax.nn.sigmoid(Pg[..., a_off + GDN_H:])
    o, s_gdn = _gated_delta_chunked(q, k, v, g_log, beta, st_gdn.astype(f32))
    y_gdn = (_rms_last(o) * p['gdn_norm_w']).reshape(B, T, BR_WIDTH) * jax.nn.silu(Pg[..., o3:a_off])

    Ph = proj[..., OFF_HG:OFF_GATE]
    qh = jax.nn.silu(Ph[..., :o1])
    ls = jax.nn.log_sigmoid(Ph[..., o1:o2])
    lb_pos = lb > 0.0
    log_lb = jnp.log(jnp.where(lb_pos, lb, 1.0))
    log_f = jnp.where(lb_pos, jnp.logaddexp(log_lb, jnp.log1p(-lb) + ls), ls)
    kh = -jnp.expm1(log_f)
    o, s_hg = _gla_chunked(_heads(qh, HG_H), _heads(kh, HG_H), _heads(Ph[..., o2:o3], HG_H),
                           _heads(log_f, HG_H), st_hg.astype(f32))
    y_hg = _rms_last(o).reshape(B, T, BR_WIDTH) * p['hg_norm_w'] * jax.nn.sigmoid(Ph[..., o3:])

    gates = jax.nn.sigmoid(proj[..., OFF_GATE:])
    merged = None
    for n, y_b in enumerate((y_rw, y_ret, y_gdn, y_hg)):
        term = gates[..., n * D_MODEL:(n + 1) * D_MODEL] * (y_b @ p['w_branch'][n])
        merged = term if merged is None else merged + term
    out = merged.astype(h.dtype) @ p['w_out']
    new = (s_rw, P[:, -1:], s_ret, s_gdn, new_cv, s_hg)
    return out, tuple(s.astype(t.dtype) for s, t in zip(new, st))


def _block(x, c, st, pos, lb, p):
    mod = jax.nn.silu(c) @ p['ada_w'] + p['ada_b']
    sh1, sc1, g1, sh2, sc2, g2 = jnp.split(mod[:, None, :], 6, axis=-1)
    h = _rmsnorm(x, p['norm_mix_w']) * (1.0 + sc1) + sh1
    mix, new_st = _mixer(h, st, pos, lb, p)
    x = x + g1 * mix
    h = _rmsnorm(x, p['norm_ffn_w']) * (1.0 + sc2) + sh2
    gu = h @ p['w_gate_up']
    x = x + g2 * ((jax.nn.silu(gu[..., :D_FF]) * gu[..., D_FF:]) @ p['w_down'])
    return x, new_st


def setup_inputs(seed: int = 0) -> dict:
    key = jax.random.key(seed)
    ks = iter(jax.random.split(key, 48))
    f32 = jnp.float32

    def nrm(shape, scale):
        return scale * jax.random.normal(next(ks), shape, f32)

    def uni(shape, lo, hi):
        return jax.random.uniform(next(ks), shape, f32, lo, hi)

    dt = jnp.exp(uni((DEPTH, GDN_H), math.log(1e-3), math.log(1e-1)))
    return {
        'x_prompt': nrm((BATCH, SEQ, D_MODEL), 1.0),
        'x_sample': nrm((DEC_BATCH, DEC_SEQ, D_MODEL), 1.0),
        'c_prompt': nrm((BATCH, D_MODEL), 1.0),
        'c_sample': nrm((DEC_BATCH, D_MODEL), 1.0),
        'state_rwkv': nrm((DEPTH, DEC_BATCH, RW_H, HEAD_DIM, HEAD_DIM), 0.3),
        'state_rwkv_shift': nrm((DEPTH, DEC_BATCH, 1, RW_COLS), 1.0),
        'state_ret': nrm((DEPTH, DEC_BATCH, RET_H, HEAD_DIM, HEAD_DIM), 0.3),
        'state_gdn': nrm((DEPTH, DEC_BATCH, GDN_H, HEAD_DIM, HEAD_DIM), 0.3),
        'state_gdn_conv': nrm((DEPTH, DEC_BATCH, GDN_CONV - 1, GDN_CONV_CH), 1.0),
        'state_hgrn': nrm((DEPTH, DEC_BATCH, HG_H, HEAD_DIM, HEAD_DIM), 0.3),
        'ada_w': nrm((DEPTH, D_MODEL, 6 * D_MODEL), 0.5 * D_MODEL ** -0.5),
        'ada_b': nrm((DEPTH, 6 * D_MODEL), 0.02),
        'norm_mix_w': 1.0 + nrm((DEPTH, D_MODEL), 0.02),
        'w_in': nrm((DEPTH, D_MODEL, IN_COLS), D_MODEL ** -0.5),
        'rw_mu': uni((DEPTH, RW_COLS), 0.0, 1.0),
        'rw_w0': uni((DEPTH, BR_WIDTH), -6.0, -1.0),
        'rw_w2': nrm((DEPTH, RW_LORA_W, BR_WIDTH), 0.1),
        'rw_a0': nrm((DEPTH, BR_WIDTH), 0.1),
        'rw_a2': nrm((DEPTH, RW_LORA_A, BR_WIDTH), 0.1),
        'rw_g2': nrm((DEPTH, RW_LORA_G, BR_WIDTH), RW_LORA_G ** -0.5),
        'rw_k_k': 0.85 + nrm((DEPTH, BR_WIDTH), 0.02),
        'rw_k_a': 1.0 + nrm((DEPTH, BR_WIDTH), 0.02),
        'rw_r_k': nrm((DEPTH, RW_H, HEAD_DIM), 0.1),
        'rw_ln_w': 1.0 + nrm((DEPTH, BR_WIDTH), 0.02),
        'rw_ln_b': nrm((DEPTH, BR_WIDTH), 0.02),
        'gdn_conv_w': nrm((DEPTH, GDN_CONV, GDN_CONV_CH), 0.5),
        'gdn_a_log': jnp.log(uni((DEPTH, GDN_H), 1.0, 16.0)),
        'gdn_dt_bias': dt + jnp.log(-jnp.expm1(-dt)),
        'gdn_norm_w': 1.0 + nrm((DEPTH, HEAD_DIM), 0.02),
        'hg_lb_logits': nrm((DEPTH, BR_WIDTH), 1.0),
        'hg_norm_w': 1.0 + nrm((DEPTH, BR_WIDTH), 0.02),
        'w_branch': nrm((DEPTH, N_BRANCH, BR_WIDTH, D_MODEL), BR_WIDTH ** -0.5),
        'w_out': nrm((DEPTH, D_MODEL, D_MODEL), D_MODEL ** -0.5),
        'norm_ffn_w': 1.0 + nrm((DEPTH, D_MODEL), 0.02),
        'w_gate_up': nrm((DEPTH, D_MODEL, 2 * D_FF), D_MODEL ** -0.5),
        'w_down': nrm((DEPTH, D_FF, D_MODEL), D_FF ** -0.5),
        'final_norm_w': 1.0 + nrm((D_MODEL,), 0.02),
    }


def reference(x_prompt, x_sample, c_prompt, c_sample, state_rwkv, state_rwkv_shift, state_ret, state_gdn,
              state_gdn_conv, state_hgrn, ada_w, ada_b, norm_mix_w, w_in, rw_mu, rw_w0, rw_w2, rw_a0, rw_a2,
              rw_g2, rw_k_k, rw_k_a, rw_r_k, rw_ln_w, rw_ln_b, gdn_conv_w, gdn_a_log, gdn_dt_bias, gdn_norm_w,
              hg_lb_logits, hg_norm_w, w_branch, w_out, norm_ffn_w, w_gate_up, w_down, final_norm_w):
    lb_p = jax.nn.softmax(hg_lb_logits.astype(jnp.float32), axis=0)
    lower_bounds = jnp.cumsum(lb_p, axis=0) - lb_p[0]
    Bp, Tp = x_prompt.shape[0], x_prompt.shape[1]
    pos_p = jnp.arange(Tp)
    pos_s = PAST_LEN + jnp.arange(x_sample.shape[1])
    dt = x_prompt.dtype
    zero_st = (jnp.zeros((Bp, RW_H, HEAD_DIM, HEAD_DIM), dt), jnp.zeros((Bp, 1, RW_COLS), dt),
               jnp.zeros((Bp, RET_H, HEAD_DIM, HEAD_DIM), dt), jnp.zeros((Bp, GDN_H, HEAD_DIM, HEAD_DIM), dt),
               jnp.zeros((Bp, GDN_CONV - 1, GDN_CONV_CH), dt), jnp.zeros((Bp, HG_H, HEAD_DIM, HEAD_DIM), dt))
    new_p = ([], [], [], [], [], [])
    new_s = ([], [], [], [], [], [])
    xp, xs = x_prompt, x_sample
    for l in range(DEPTH):
        p = {'ada_w': ada_w[l], 'ada_b': ada_b[l], 'norm_mix_w': norm_mix_w[l], 'w_in': w_in[l],
             'rw_mu': rw_mu[l], 'rw_w0': rw_w0[l], 'rw_w2': rw_w2[l], 'rw_a0': rw_a0[l], 'rw_a2': rw_a2[l],
             'rw_g2': rw_g2[l], 'rw_k_k': rw_k_k[l], 'rw_k_a': rw_k_a[l], 'rw_r_k': rw_r_k[l],
             'rw_ln_w': rw_ln_w[l], 'rw_ln_b': rw_ln_b[l], 'gdn_conv_w': gdn_conv_w[l],
             'gdn_a_log': gdn_a_log[l], 'gdn_dt_bias': gdn_dt_bias[l], 'gdn_norm_w': gdn_norm_w[l],
             'hg_norm_w': hg_norm_w[l], 'w_branch': w_branch[l], 'w_out': w_out[l],
             'norm_ffn_w': norm_ffn_w[l], 'w_gate_up': w_gate_up[l], 'w_down': w_down[l]}
        xp, stp = _block(xp, c_prompt, zero_st, pos_p, lower_bounds[l], p)
        st_l = (state_rwkv[l], state_rwkv_shift[l], state_ret[l], state_gdn[l], state_gdn_conv[l], state_hgrn[l])
        xs, sts = _block(xs, c_sample, st_l, pos_s, lower_bounds[l], p)
        for lst, s in zip(new_p, stp):
            lst.append(s)
        for lst, s in zip(new_s, sts):
            lst.append(s)
    y_prompt = _rmsnorm(xp, final_norm_w)
    y_sample = _rmsnorm(xs, final_norm_w)
    p_rw, p_sh, p_ret, p_gdn, p_cv, p_hg = (jnp.stack(lst) for lst in new_p)
    s_rw, s_sh, s_ret, s_gdn, s_cv, s_hg = (jnp.stack(lst) for lst in new_s)
    return (y_prompt, y_sample, p_rw, p_sh, p_ret, p_gdn, p_cv, p_hg, s_rw, s_sh, s_ret, s_gdn, s_cv, s_hg)
```

```python
import functools
import math

import jax
import jax.numpy as jnp
from jax import lax
from jax.experimental import pallas as pl
from jax.experimental.pallas import tpu as pltpu

F32 = jnp.float32
BF = jnp.bfloat16

D_MODEL = 1024
DEPTH = 4
PAST_LEN = 16384
BR = 256
HD = 64
NH = 4
RW_LORA_W = 64
RW_LORA_A = 64
RW_GN_EPS = 64e-5
ROPE_BASE = 10000.0
GDN_CONV = 4
CONV_CH = 3 * BR
D_FF = 2816
NORM_EPS = 1e-6
CHUNK = 64
SUB = 16
LANES = 128
VMEM_LIMIT = 56 * 1024 * 1024

NN = ((1,), (0,))
NT = ((1,), (1,))
TN = ((0,), (0,))


def _mm(a, b, dims=NN):
    return lax.dot_general(a.astype(BF), b.astype(BF), (dims, ((), ())), preferred_element_type=F32)


def _split2(a):
    hi = a.astype(BF)
    lo = (a - hi.astype(F32)).astype(BF)
    return hi, lo


def _split3(a):
    hi = a.astype(BF)
    r = a - hi.astype(F32)
    mid = r.astype(BF)
    lo = (r - mid.astype(F32)).astype(BF)
    return hi, mid, lo


def _mm3(a, b, dims=NN):
    ah, al = _split2(a)
    bh, bl = _split2(b)
    return _mm(ah, bh, dims) + (_mm(ah, bl, dims) + _mm(al, bh, dims))


def _mm_exact_rhs(a, b, dims=NN, parts=3):
    ps = _split3(a) if parts == 3 else _split2(a)
    out = _mm(ps[0], b, dims)
    for p in ps[1:]:
        out = out + _mm(p, b, dims)
    return out


def _mm_exact_lhs(a, b, dims=NN, parts=3):
    ps = _split3(b) if parts == 3 else _split2(b)
    out = _mm(a, ps[0], dims)
    for p in ps[1:]:
        out = out + _mm(a, p, dims)
    return out


def _sigmoid(x):
    return jax.nn.sigmoid(x)


def _silu(x):
    return x * jax.nn.sigmoid(x)


def _softplus(x):
    return jnp.maximum(x, 0.0) + jnp.log1p(jnp.exp(-jnp.abs(x)))


def _head_sum(x, ones_bd):
    return _mm_exact_rhs(x, ones_bd, NN, parts=2)


def _tri_inverse(L, row, col):
    c = L.shape[0]
    eye = (row == col).astype(F32)
    bd = (row // SUB) == (col // SUB)
    Ld = jnp.where(bd, L, 0.0)
    Dm = eye - Ld
    P = Ld
    for _ in range(int(math.log2(SUB)) - 1):
        P = _mm3(P, P)
        Dm = Dm + _mm3(Dm, P)
    size = SUB
    while size < c:
        off = ((row // (2 * size)) == (col // (2 * size))) & ((row // size) != (col // size))
        C = jnp.where(off, L, 0.0)
        Dm = Dm - _mm3(_mm3(Dm, C), Dm)
        size *= 2
    return Dm


def _mod_body(c_ref, w_ref, b_ref, o_ref):
    o_ref[0] = _mm(_silu(c_ref[...]), w_ref[0]) + b_ref[0]


def _ada_mod(c_all, ada_w, ada_b):
    n = c_all.shape[0]
    ncol = ada_w.shape[2] // D_MODEL
    return pl.pallas_call(
        _mod_body,
        grid=(DEPTH, ncol),
        in_specs=[
            pl.BlockSpec((n, D_MODEL), lambda l, j: (0, 0)),
            pl.BlockSpec((1, D_MODEL, D_MODEL), lambda l, j: (l, 0, j)),
            pl.BlockSpec((1, 1, D_MODEL), lambda l, j: (l, 0, j)),
        ],
        out_specs=pl.BlockSpec((1, n, D_MODEL), lambda l, j: (l, 0, j)),
        out_shape=jax.ShapeDtypeStruct((DEPTH, n, ncol * D_MODEL), F32),
        compiler_params=pltpu.CompilerParams(dimension_semantics=("arbitrary", "arbitrary"),
                                             vmem_limit_bytes=VMEM_LIMIT),
        name="ada_mod",
    )(c_all, ada_w, ada_b.reshape(DEPTH, 1, -1))


def _lb_body(x_ref, o_ref):
    x = x_ref[...]
    m = jnp.max(x, axis=0, keepdims=True)
    e = jnp.exp(x - m)
    p = e / jnp.sum(e, axis=0, keepdims=True)
    acc = jnp.zeros_like(p[0:1])
    rows = [acc]
    for l in range(1, DEPTH):
        acc = acc + p[l:l + 1]
        rows.append(acc)
    o_ref[...] = jnp.concatenate(rows, axis=0)


def _lower_bounds(logits):
    return pl.pallas_call(
        _lb_body,
        out_shape=jax.ShapeDtypeStruct(logits.shape, F32),
        name="hgrn_lower_bounds",
    )(logits.astype(F32))


def _mod_spec(rm, tm, col):
    if rm == 1:
        return pl.BlockSpec((1, 1, D_MODEL), lambda b, i: (b, 0, col))
    return pl.BlockSpec((1, tm, D_MODEL), lambda b, i: (b, i, col))


def _full_spec(shape):
    nd = len(shape)
    return pl.BlockSpec(shape, lambda b, i: (0,) * nd)


def _row_spec(tm, width):
    return pl.BlockSpec((1, tm, width), lambda b, i: (b, i, 0))


def _modulated_norm(x, nw, sc, sh):
    ms = jnp.mean(x * x, axis=-1, keepdims=True)
    return (x * lax.rsqrt(ms + NORM_EPS) * nw) * (1.0 + sc) + sh


def _in_body(x_ref, sc_ref, sh_ref, nw_ref, wrw_ref, wret_ref, wgdn_ref, wab_ref, whg_ref,
             h_ref, prw_ref, pret_ref, pgdn_ref, pab_ref, phg_ref):
    h = _modulated_norm(x_ref[0], nw_ref[...], sc_ref[0], sh_ref[0]).astype(BF)
    h_ref[0] = h
    prw_ref[0] = _mm(h, wrw_ref[...])
    pret_ref[0] = _mm(h, wret_ref[...])
    pgdn_ref[0] = _mm(h, wgdn_ref[...])
    pab_ref[0] = _mm(h, wab_ref[...])
    phg_ref[0] = _mm(h, whg_ref[...])


def _in_proj(x, mod, rm, tm, nw, ws):
    G, R, _ = x.shape
    wrw, wret, wgdn, wab, whg = ws
    outs = [jax.ShapeDtypeStruct((G, R, D_MODEL), BF)] + [
        jax.ShapeDtypeStruct((G, R, w.shape[1]), F32) for w in ws]
    return pl.pallas_call(
        _in_body,
        grid=(G, R // tm),
        in_specs=[_row_spec(tm, D_MODEL), _mod_spec(rm, tm, 1), _mod_spec(rm, tm, 0),
                  _full_spec((1, D_MODEL))] + [_full_spec(w.shape) for w in ws],
        out_specs=[_row_spec(tm, D_MODEL)] + [_row_spec(tm, w.shape[1]) for w in ws],
        out_shape=outs,
        compiler_params=pltpu.CompilerParams(dimension_semantics=("arbitrary", "arbitrary"),
                                             vmem_limit_bytes=VMEM_LIMIT),
        name="in_proj",
    )(x, mod, mod, nw, wrw, wret, wgdn, wab, whg)


def _merge_body(x_ref, h_ref, yrw_ref, yret_ref, ygdn_ref, yhg_ref, g1_ref, wg_ref, wb_ref, wo_ref,
                o_ref):
    h = h_ref[0]
    merged = None
    for n, y_ref in enumerate((yrw_ref, yret_ref, ygdn_ref, yhg_ref)):
        gate = _sigmoid(_mm(h, wg_ref[:, n * D_MODEL:(n + 1) * D_MODEL]))
        term = gate * _mm(y_ref[0], wb_ref[n])
        merged = term if merged is None else merged + term
    o_ref[0] = x_ref[0] + g1_ref[0] * _mm(merged, wo_ref[...])


def _merge(x, h, ys, mod, rm, tm, wg, wb, wo):
    G, R, _ = x.shape
    return pl.pallas_call(
        _merge_body,
        grid=(G, R // tm),
        in_specs=[_row_spec(tm, D_MODEL), _row_spec(tm, D_MODEL)] + [_row_spec(tm, BR)] * 4
        + [_mod_spec(rm, tm, 2), _full_spec(wg.shape), _full_spec(wb.shape), _full_spec(wo.shape)],
        out_specs=_row_spec(tm, D_MODEL),
        out_shape=jax.ShapeDtypeStruct((G, R, D_MODEL), F32),
        compiler_params=pltpu.CompilerParams(dimension_semantics=("arbitrary", "arbitrary"),
                                             vmem_limit_bytes=VMEM_LIMIT),
        name="merge_out",
    )(x, h, *ys, mod, wg, wb, wo)


def _ffn_body(final, x_ref, sc_ref, sh_ref, g2_ref, nw_ref, wgu_ref, wd_ref, fw_ref, o_ref):
    x = x_ref[0]
    h = _modulated_norm(x, nw_ref[...], sc_ref[0], sh_ref[0]).astype(BF)
    gate = _mm(h, wgu_ref[:, :D_FF])
    up = _mm(h, wgu_ref[:, D_FF:])
    x = x + g2_ref[0] * _mm(_silu(gate) * up, wd_ref[...])
    if final:
        ms = jnp.mean(x * x, axis=-1, keepdims=True)
        x = x * lax.rsqrt(ms + NORM_EPS) * fw_ref[...]
    o_ref[0] = x


def _ffn(x, mod, rm, tm, nw, wgu, wd, fw, final):
    G, R, _ = x.shape
    return pl.pallas_call(
        functools.partial(_ffn_body, final),
        grid=(G, R // tm),
        in_specs=[_row_spec(tm, D_MODEL), _mod_spec(rm, tm, 4), _mod_spec(rm, tm, 3), _mod_spec(rm, tm, 5),
                  _full_spec((1, D_MODEL)), _full_spec(wgu.shape), _full_spec(wd.shape),
                  _full_spec((1, D_MODEL))],
        out_specs=_row_spec(tm, D_MODEL),
        out_shape=jax.ShapeDtypeStruct((G, R, D_MODEL), F32),
        compiler_params=pltpu.CompilerParams(dimension_semantics=("arbitrary", "arbitrary"),
                                             vmem_limit_bytes=VMEM_LIMIT),
        name="ffn",
    )(x, mod, mod, mod, nw, wgu, wd, fw)


def _chunk_spec(c, width):
    return pl.BlockSpec((1, c, width), lambda b, i: (b, i, 0))


def _state_spec():
    return pl.BlockSpec((1, NH, HD, HD), lambda b, i: (b, 0, 0, 0))


def _iota2(shape):
    return (lax.broadcasted_iota(jnp.int32, shape, 0), lax.broadcasted_iota(jnp.int32, shape, 1))


def _rope(t, cosf, sins):
    lane = lax.broadcasted_iota(jnp.int32, t.shape, 1)
    width = t.shape[1]
    rot = jnp.where((lane % HD) < HD // 2, pltpu.roll(t, width - HD // 2, 1), pltpu.roll(t, HD // 2, 1))
    return t * cosf + rot * sins


def _ret_body(p_ref, cos_ref, sin_ref, qin_ref, kout_ref, dmat_ref, cdec_ref, ones_ref, y_ref, s_ref):
    @pl.when(pl.program_id(1) == 0)
    def _():
        s_ref[...] = jnp.zeros_like(s_ref)

    P = p_ref[0]
    cosf, sins = cos_ref[...], sin_ref[...]
    q = _rope(P[:, :BR], cosf, sins)
    k = _rope(P[:, BR:2 * BR], cosf, sins) * HD ** -0.5
    v = P[:, 2 * BR:3 * BR]
    g = P[:, 3 * BR:]
    qs = q * qin_ref[...]
    ko = k * kout_ref[...]
    outs = []
    for h in range(NH):
        sl = slice(h * HD, (h + 1) * HD)
        S = s_ref[0, h]
        s = _mm(q[:, sl], k[:, sl], NT) * dmat_ref[h]
        outs.append(_mm(s, v[:, sl]) + _mm(qs[:, sl], S))
        s_ref[0, h] = S * cdec_ref[h] + _mm(ko[:, sl], v[:, sl], TN)
    o = jnp.concatenate(outs, axis=1)
    ms = _head_sum(o * o, ones_ref[...]) * (1.0 / HD)
    y_ref[0] = (o * lax.rsqrt(ms + NORM_EPS) * _silu(g)).astype(BF)


def _ret_tables(c):
    lg = jnp.log1p(-jnp.exp2(-5.0 - jnp.arange(NH, dtype=F32)))[:, None]
    idx = jnp.arange(c, dtype=F32)
    diff = idx[:, None] - idx[None, :]
    dmat = jnp.where(diff >= 0, jnp.exp(lg[:, :, None] * jnp.maximum(diff, 0.0)), 0.0)
    q_in = jnp.exp(lg * (idx + 1.0))
    k_out = jnp.exp(lg * (c - 1.0 - idx))
    c_dec = jnp.exp(lg * c)
    expand = lambda t: jnp.repeat(t.T, HD, axis=1)
    return dmat, expand(q_in), expand(k_out), jnp.broadcast_to(c_dec[:, :, None], (NH, HD, HD))


def _rope_tables(pos):
    half = HD // 2
    freqs = ROPE_BASE ** (-jnp.arange(half, dtype=F32) / half)
    ang = pos.astype(F32)[:, None] * freqs
    cos, sin = jnp.cos(ang), jnp.sin(ang)
    cosf = jnp.tile(jnp.concatenate([cos, cos], axis=1), (1, NH))
    sins = jnp.tile(jnp.concatenate([-sin, sin], axis=1), (1, NH))
    return cosf, sins


def _ret_prompt(p, consts):
    B, T, _ = p.shape
    c = CHUNK
    cosf, sins = _rope_tables(jnp.arange(T))
    dmat, q_in, k_out, c_dec = _ret_tables(c)
    return pl.pallas_call(
        _ret_body,
        grid=(B, T // c),
        in_specs=[_chunk_spec(c, D_MODEL),
                  pl.BlockSpec((c, BR), lambda b, i: (i, 0)), pl.BlockSpec((c, BR), lambda b, i: (i, 0)),
                  _full_spec((c, BR)), _full_spec((c, BR)), _full_spec((NH, c, c)), _full_spec((NH, HD, HD)),
                  _full_spec((BR, BR))],
        out_specs=[_chunk_spec(c, BR), _state_spec()],
        out_shape=[jax.ShapeDtypeStruct((B, T, BR), BF), jax.ShapeDtypeStruct((B, NH, HD, HD), F32)],
        compiler_params=pltpu.CompilerParams(dimension_semantics=("arbitrary", "arbitrary"),
                                             vmem_limit_bytes=VMEM_LIMIT),
        name="ret_prompt",
    )(p, cosf, sins, q_in, k_out, dmat, c_dec, consts["ones_bd"])


def _gdn_body(pg_ref, pab_ref, cw_ref, alog_ref, dtb_ref, nw_ref, tri_ref, triu_ref, ones_ref,
              y_ref, s_ref, cv_ref, xext):
    c = CHUNK

    @pl.when(pl.program_id(1) == 0)
    def _():
        s_ref[...] = jnp.zeros_like(s_ref)
        xext[pl.ds(0, 8), :] = jnp.zeros((8, CONV_CH), F32)

    P = pg_ref[0]
    X = P[:, :CONV_CH]
    xext[pl.ds(8, c), :] = X
    cw = cw_ref[...]
    conv = xext[pl.ds(8 - 3, c), :] * cw[0:1]
    conv = conv + xext[pl.ds(8 - 2, c), :] * cw[1:2]
    conv = conv + xext[pl.ds(8 - 1, c), :] * cw[2:3]
    conv = conv + X * cw[3:4]
    xext[pl.ds(0, 8), :] = X[c - 8:c]
    cv_ref[0] = X[c - (GDN_CONV - 1):c]
    conv = _silu(conv)
    ones_bd = ones_ref[...]
    qn, kn, v = conv[:, :BR], conv[:, BR:2 * BR], conv[:, 2 * BR:]
    q = qn * lax.rsqrt(_head_sum(qn * qn, ones_bd) + 1e-6) * HD ** -0.5
    k = kn * lax.rsqrt(_head_sum(kn * kn, ones_bd) + 1e-6)
    gate = P[:, CONV_CH:]

    ab = pab_ref[0]
    glog = -jnp.exp(alog_ref[...]) * _softplus(ab + dtb_ref[...])
    beta = _sigmoid(ab)
    Gc = _mm_exact_lhs(tri_ref[...], glog, NN)
    Gr = _mm_exact_rhs(glog, triu_ref[...], TN)
    row, col = _iota2((c, c))
    incl = row >= col
    strict = row > col
    outs = []
    for h in range(NH):
        sl = slice(h * HD, (h + 1) * HD)
        qh, kh, vh = q[:, sl], k[:, sl], v[:, sl]
        gc = Gc[:, h:h + 1]
        gr = Gr[h:h + 1, :]
        bcol = beta[:, NH + h:NH + h + 1]
        gam = jnp.where(incl, jnp.exp(jnp.where(incl, gc - gr, 0.0)), 0.0)
        L = jnp.where(strict, bcol * gam * _mm(kh, kh, NT), 0.0)
        eg = jnp.exp(gc)
        rhs = jnp.concatenate([vh * bcol, kh * (bcol * eg)], axis=1)
        sol = _mm3(_tri_inverse(L, row, col), rhs)
        u_v, w_k = sol[:, :HD], sol[:, HD:]
        qk = _mm(qh, kh, NT) * gam
        S = s_ref[0, h]
        u = u_v - _mm(w_k, S)
        outs.append(_mm(qh * eg, S) + _mm(qk, u))
        gend = gc[c - 1:c, :]
        kout = kh * jnp.exp(gend - gc)
        s_ref[0, h] = S * jnp.exp(gend) + _mm(kout, u, TN)
    o = jnp.concatenate(outs, axis=1)
    ms = _head_sum(o * o, ones_bd) * (1.0 / HD)
    y_ref[0] = (o * lax.rsqrt(ms + NORM_EPS) * nw_ref[...] * _silu(gate)).astype(BF)


def _gdn_prompt(pg, pab, cw, alog, dtb, nw, consts):
    B, T, _ = pg.shape
    c = CHUNK
    return pl.pallas_call(
        _gdn_body,
        grid=(B, T // c),
        in_specs=[_chunk_spec(c, D_MODEL), _chunk_spec(c, LANES), _full_spec((GDN_CONV, CONV_CH)),
                  _full_spec((1, LANES)), _full_spec((1, LANES)), _full_spec((1, BR)),
                  _full_spec((c, c)), _full_spec((c, c)), _full_spec((BR, BR))],
        out_specs=[_chunk_spec(c, BR), _state_spec(),
                   pl.BlockSpec((1, GDN_CONV - 1, CONV_CH), lambda b, i: (b, 0, 0))],
        out_shape=[jax.ShapeDtypeStruct((B, T, BR), BF), jax.ShapeDtypeStruct((B, NH, HD, HD), F32),
                   jax.ShapeDtypeStruct((B, GDN_CONV - 1, CONV_CH), F32)],
        scratch_shapes=[pltpu.VMEM((c + 8, CONV_CH), F32)],
        compiler_params=pltpu.CompilerParams(dimension_semantics=("arbitrary", "arbitrary"),
                                             vmem_limit_bytes=VMEM_LIMIT),
        name="gdn_prompt",
    )(pg, pab, cw, alog, dtb, nw, consts["tri"], consts["triu"], consts["ones_bd"])


def _hgrn_gates(xq, xf, lb):
    q = _silu(xq)
    ls = jnp.minimum(xf, 0.0) - jnp.log1p(jnp.exp(-jnp.abs(xf)))
    pos = lb > 0.0
    a = jnp.log(jnp.where(pos, lb, 1.0))
    b = jnp.log1p(-lb) + ls
    lae = jnp.maximum(a, b) + jnp.log1p(jnp.exp(-jnp.abs(a - b)))
    log_f = jnp.where(pos, lae, ls)
    k = (1.0 - lb) * _sigmoid(-xf)
    return q, log_f, k


def _hg_offsets(c):
    offs, total = [], 0
    for j in range(c):
        offs.append(total)
        total += c - (j // 8) * 8
    return offs, total


def _hg_body(p_ref, lb_ref, nw_ref, tri_ref, ones_ref, eye_ref, y_ref, s_ref, st, tbuf, sbuf):
    c = CHUNK
    ci = pl.program_id(1)

    @pl.when(ci == 0)
    def _():
        st[...] = jnp.zeros_like(st)

    P = p_ref[0]
    q, log_f, k = _hgrn_gates(P[:, :BR], P[:, BR:2 * BR], lb_ref[...])
    v = P[:, 2 * BR:3 * BR]
    og = P[:, 3 * BR:]
    G = _mm_exact_lhs(tri_ref[...], log_f, NN)
    offs, total = _hg_offsets(c)
    rowi = lax.broadcasted_iota(jnp.int32, (8, BR), 0)
    for j in range(c):
        i0 = (j // 8) * 8
        d = G[i0:, :] - G[j:j + 1, :]
        e = jnp.exp(jnp.minimum(d, 0.0))
        t = q[i0:, :] * e * k[j:j + 1, :]
        if j % 8:
            head = jnp.where(rowi >= (j - i0), t[0:8], 0.0)
            t = jnp.concatenate([head, t[8:]], axis=0) if c - i0 > 8 else head
        tbuf[pl.ds(offs[j], c - i0), :] = t
    sbuf[...] = _head_sum(tbuf[...], ones_ref[...])
    blocks = [jnp.zeros((8, BR), F32) for _ in range(c // 8)]
    for j in range(c):
        i0 = (j // 8) * 8
        contrib = sbuf[pl.ds(offs[j], c - i0), :] * v[j:j + 1, :]
        for ib in range(i0 // 8, c // 8):
            blocks[ib] = blocks[ib] + contrib[ib * 8 - i0:ib * 8 - i0 + 8]
    o = jnp.concatenate(blocks, axis=0)
    qg = q * jnp.exp(G)
    gend = G[c - 1:c, :]
    kd = k * jnp.exp(gend - G)
    dec = jnp.exp(gend)
    outs = []
    for h in range(NH):
        sl = slice(h * HD, (h + 1) * HD)
        ST = st[h]
        outs.append(_mm(qg[:, sl], ST, NT))
        st[h] = ST * dec[:, sl] + _mm(v[:, sl], kd[:, sl], TN)
    o = o + jnp.concatenate(outs, axis=1)
    ms = _head_sum(o * o, ones_ref[...]) * (1.0 / HD)
    y_ref[0] = (o * lax.rsqrt(ms + NORM_EPS) * nw_ref[...] * _sigmoid(og)).astype(BF)

    @pl.when(ci == pl.num_programs(1) - 1)
    def _():
        for h in range(NH):
            s_ref[0, h] = _mm_exact_lhs(eye_ref[...], st[h], NT)


def _hg_prompt(p, lb, nw, consts):
    B, T, _ = p.shape
    c = CHUNK
    _, total = _hg_offsets(c)
    return pl.pallas_call(
        _hg_body,
        grid=(B, T // c),
        in_specs=[_chunk_spec(c, D_MODEL), _full_spec((1, BR)), _full_spec((1, BR)), _full_spec((c, c)),
                  _full_spec((BR, BR)), _full_spec((HD, HD))],
        out_specs=[_chunk_spec(c, BR), _state_spec()],
        out_shape=[jax.ShapeDtypeStruct((B, T, BR), BF), jax.ShapeDtypeStruct((B, NH, HD, HD), F32)],
        scratch_shapes=[pltpu.VMEM((NH, HD, HD), F32), pltpu.VMEM((total, BR), F32),
                        pltpu.VMEM((total, BR), F32)],
        compiler_params=pltpu.CompilerParams(dimension_semantics=("arbitrary", "arbitrary"),
                                             vmem_limit_bytes=VMEM_LIMIT),
        name="hgrn_prompt",
    )(p, lb, nw, consts["tri"], consts["ones_bd"], consts["eye"])


def _rw_features(P, prev, mu, w0, w2, a0, a2, g2, kkw, kaw, ones_bd):
    Pm = P + mu * (prev - P)
    r, kx, vx = Pm[:, :BR], Pm[:, BR:2 * BR], Pm[:, 2 * BR:3 * BR]
    o4 = 3 * BR + RW_LORA_W
    o5 = o4 + RW_LORA_A
    wl, al, gl = Pm[:, 3 * BR:o4], Pm[:, o4:o5], Pm[:, o5:]
    zw = w0 + _mm(jnp.tanh(wl), w2)
    lw = -jnp.exp(-_softplus(-zw) - 0.5)
    a = _sigmoid(a0 + _mm(al, a2))
    g = _mm(_sigmoid(gl), g2)
    kks = kx * kkw
    kk = kks * lax.rsqrt(_head_sum(kks * kks, ones_bd) + 1e-6)
    k = kx * (1.0 + (a - 1.0) * kaw)
    return r, lw, k, vx, -kk, kk * a, g


def _rw_output(o, r, k, vx, g, rkw, lnw, lnb, ones_bd):
    mean = _head_sum(o, ones_bd) * (1.0 / HD)
    cen = o - mean
    var = _head_sum(cen * cen, ones_bd) * (1.0 / HD)
    on = cen * lax.rsqrt(var + RW_GN_EPS) * lnw + lnb
    bonus = _head_sum(r * k * rkw, ones_bd) * vx
    return (on + bonus) * g


def _rw_body(p_ref, mu_ref, w0_ref, w2_ref, a0_ref, a2_ref, g2_ref, kkw_ref, kaw_ref, rkw_ref,
             lnw_ref, lnb_ref, tri_ref, ones_ref, eye_ref, y_ref, s_ref, sh_ref, st, prev):
    c = CHUNK
    ci = pl.program_id(1)

    @pl.when(ci == 0)
    def _():
        st[...] = jnp.zeros_like(st)
        prev[...] = jnp.zeros_like(prev)

    P = p_ref[0]
    rowp = lax.broadcasted_iota(jnp.int32, P.shape, 0)
    shifted = jnp.where(rowp == 0, prev[0:1, :], pltpu.roll(P, 1, 0))
    prev[0:1, :] = P[c - 1:c, :]
    sh_ref[0] = P[c - 1:c, :]
    ones_bd = ones_ref[...]
    r, lw, k, vx, av, bv, g = _rw_features(P, shifted, mu_ref[...], w0_ref[...], w2_ref[...], a0_ref[...],
                                           a2_ref[...], g2_ref[...], kkw_ref[...], kaw_ref[...], ones_bd)
    LG = _mm_exact_lhs(tri_ref[...], lw, NN)
    rt = r * jnp.exp(LG)
    at = av * jnp.exp(LG - lw)
    einv = jnp.exp(-LG)
    kh_ = k * einv
    bh_ = bv * einv
    lgend = LG[c - 1:c, :]
    eo = jnp.exp(lgend - LG)
    kout = k * eo
    bout = bv * eo
    dec = jnp.exp(lgend)
    row, col = _iota2((c, c))
    incl = row >= col
    strict = row > col
    outs = []
    for h in range(NH):
        sl = slice(h * HD, (h + 1) * HD)
        ST = st[h]
        X = jnp.concatenate([at[:, sl], rt[:, sl]], axis=0)
        Y = jnp.concatenate([bh_[:, sl], kh_[:, sl]], axis=0)
        M = _mm(X, Y, NT)
        Aab = jnp.where(strict, M[:c, :c], 0.0)
        Aak = jnp.where(strict, M[:c, c:], 0.0)
        Arb = jnp.where(incl, M[c:, :c], 0.0)
        Ark = jnp.where(incl, M[c:, c:], 0.0)
        XS = _mm(X, ST, NT)
        vh = vx[:, sl]
        rhs = _mm(Aak, vh) + XS[:c]
        U = _mm3(_tri_inverse(-Aab, row, col), rhs)
        outs.append(XS[c:] + _mm(Arb, U) + _mm(Ark, vh))
        st[h] = ST * dec[:, sl] + _mm(U, bout[:, sl], TN) + _mm(vh, kout[:, sl], TN)
    o = jnp.concatenate(outs, axis=1)
    y_ref[0] = _rw_output(o, r, k, vx, g, rkw_ref[...], lnw_ref[...], lnb_ref[...], ones_bd).astype(BF)

    @pl.when(ci == pl.num_programs(1) - 1)
    def _():
        for h in range(NH):
            s_ref[0, h] = _mm_exact_lhs(eye_ref[...], st[h], NT)


def _rw_prompt(p, lp, consts):
    B, T, _ = p.shape
    c = CHUNK
    params = [lp[n] for n in ("rw_mu", "rw_w0", "rw_w2", "rw_a0", "rw_a2", "rw_g2", "rw_k_k", "rw_k_a",
                              "rw_r_k", "rw_ln_w", "rw_ln_b")]
    return pl.pallas_call(
        _rw_body,
        grid=(B, T // c),
        in_specs=[_chunk_spec(c, D_MODEL)] + [_full_spec(t.shape) for t in params]
        + [_full_spec((c, c)), _full_spec((BR, BR)), _full_spec((HD, HD))],
        out_specs=[_chunk_spec(c, BR), _state_spec(),
                   pl.BlockSpec((1, 1, D_MODEL), lambda b, i: (b, 0, 0))],
        out_shape=[jax.ShapeDtypeStruct((B, T, BR), BF), jax.ShapeDtypeStruct((B, NH, HD, HD), F32),
                   jax.ShapeDtypeStruct((B, 1, D_MODEL), F32)],
        scratch_shapes=[pltpu.VMEM((NH, HD, HD), F32), pltpu.VMEM((8, D_MODEL), F32)],
        compiler_params=pltpu.CompilerParams(dimension_semantics=("arbitrary", "arbitrary"),
                                             vmem_limit_bytes=VMEM_LIMIT),
        name="rwkv_prompt",
    )(p, *params, consts["tri"], consts["ones_bd"], consts["eye"])


def _col(x):
    return x[:, :, None]


def _rowv(x):
    return x[:, None, :]


def _step_body(prw_ref, pret_ref, pgdn_ref, pab_ref, phg_ref,
               srw_ref, ssh_ref, sret_ref, sgdn_ref, scv_ref, shg_ref,
               mu_ref, w0_ref, w2_ref, a0_ref, a2_ref, g2_ref, kkw_ref, kaw_ref, rkw_ref, lnw_ref, lnb_ref,
               cos_ref, sin_ref, cw_ref, alog_ref, dtb_ref, gnw_ref, lb_ref, hnw_ref, ones_ref,
               yrw_ref, yret_ref, ygdn_ref, yhg_ref,
               nrw_ref, nsh_ref, nret_ref, ngdn_ref, ncv_ref, nhg_ref):
    ones_bd = ones_ref[...]

    P = prw_ref[...]
    nsh_ref[...] = P
    r, lw, k, vx, av, bv, g = _rw_features(P, ssh_ref[...], mu_ref[...], w0_ref[...], w2_ref[...], a0_ref[...],
                                           a2_ref[...], g2_ref[...], kkw_ref[...], kaw_ref[...], ones_bd)
    w = jnp.exp(lw)
    outs = []
    for h in range(NH):
        sl = slice(h * HD, (h + 1) * HD)
        S = srw_ref[:, h]
        sa = jnp.sum(S * _col(av[:, sl]), axis=1, keepdims=True)
        S = _col(w[:, sl]) * S + _col(bv[:, sl]) * sa + _col(k[:, sl]) * _rowv(vx[:, sl])
        nrw_ref[:, h] = S
        outs.append(jnp.sum(S * _col(r[:, sl]), axis=1))
    o = jnp.concatenate(outs, axis=1)
    yrw_ref[...] = _rw_output(o, r, k, vx, g, rkw_ref[...], lnw_ref[...], lnb_ref[...], ones_bd).astype(BF)

    P = pret_ref[...]
    cosf, sins = cos_ref[...], sin_ref[...]
    q = _rope(P[:, :BR], cosf, sins)
    k = _rope(P[:, BR:2 * BR], cosf, sins) * HD ** -0.5
    v = P[:, 2 * BR:3 * BR]
    outs = []
    for h in range(NH):
        sl = slice(h * HD, (h + 1) * HD)
        S = sret_ref[:, h] * (1.0 - 2.0 ** (-5 - h)) + _col(k[:, sl]) * _rowv(v[:, sl])
        nret_ref[:, h] = S
        outs.append(jnp.sum(S * _col(q[:, sl]), axis=1))
    o = jnp.concatenate(outs, axis=1)
    ms = _head_sum(o * o, ones_bd) * (1.0 / HD)
    yret_ref[...] = (o * lax.rsqrt(ms + NORM_EPS) * _silu(P[:, 3 * BR:])).astype(BF)

    P = pgdn_ref[...]
    X = P[:, :CONV_CH]
    cw = cw_ref[...]
    conv = scv_ref[:, 0, :] * cw[0:1] + scv_ref[:, 1, :] * cw[1:2] + scv_ref[:, 2, :] * cw[2:3] + X * cw[3:4]
    ncv_ref[:, 0, :] = scv_ref[:, 1, :]
    ncv_ref[:, 1, :] = scv_ref[:, 2, :]
    ncv_ref[:, 2, :] = X
    conv = _silu(conv)
    qn, kn, v = conv[:, :BR], conv[:, BR:2 * BR], conv[:, 2 * BR:]
    q = qn * lax.rsqrt(_head_sum(qn * qn, ones_bd) + 1e-6) * HD ** -0.5
    k = kn * lax.rsqrt(_head_sum(kn * kn, ones_bd) + 1e-6)
    ab = pab_ref[...]
    eg = jnp.exp(-jnp.exp(alog_ref[...]) * _softplus(ab + dtb_ref[...]))
    beta = _sigmoid(ab)
    outs = []
    for h in range(NH):
        sl = slice(h * HD, (h + 1) * HD)
        egh = eg[:, h:h + 1][:, :, None]
        bh = beta[:, NH + h:NH + h + 1][:, :, None]
        S = sgdn_ref[:, h]
        kS = jnp.sum(S * _col(k[:, sl]), axis=1, keepdims=True)
        u = bh * (_rowv(v[:, sl]) - egh * kS)
        S = egh * S + _col(k[:, sl]) * u
        ngdn_ref[:, h] = S
        outs.append(jnp.sum(S * _col(q[:, sl]), axis=1))
    o = jnp.concatenate(outs, axis=1)
    ms = _head_sum(o * o, ones_bd) * (1.0 / HD)
    ygdn_ref[...] = (o * lax.rsqrt(ms + NORM_EPS) * gnw_ref[...] * _silu(P[:, CONV_CH:])).astype(BF)

    P = phg_ref[...]
    q, log_f, k = _hgrn_gates(P[:, :BR], P[:, BR:2 * BR], lb_ref[...])
    f = jnp.exp(log_f)
    v = P[:, 2 * BR:3 * BR]
    outs = []
    for h in range(NH):
        sl = slice(h * HD, (h + 1) * HD)
        S = shg_ref[:, h] * _col(f[:, sl]) + _col(k[:, sl]) * _rowv(v[:, sl])
        nhg_ref[:, h] = S
        outs.append(jnp.sum(S * _col(q[:, sl]), axis=1))
    o = jnp.concatenate(outs, axis=1)
    ms = _head_sum(o * o, ones_bd) * (1.0 / HD)
    yhg_ref[...] = (o * lax.rsqrt(ms + NORM_EPS) * hnw_ref[...] * _sigmoid(P[:, 3 * BR:])).astype(BF)


def _step_sample(ps, states, lp, lb, consts, bt=8):
    prw, pret, pgdn, pab, phg = ps
    n = prw.shape[0]
    cosf, sins = _rope_tables(PAST_LEN + jnp.arange(1))
    params = [lp[nm] for nm in ("rw_mu", "rw_w0", "rw_w2", "rw_a0", "rw_a2", "rw_g2", "rw_k_k", "rw_k_a",
                                "rw_r_k", "rw_ln_w", "rw_ln_b")]
    params += [cosf, sins, lp["gdn_conv_w"], lp["gdn_a_log"], lp["gdn_dt_bias"], lp["gdn_norm_w"], lb,
               lp["hg_norm_w"], consts["ones_bd"]]
    rows = lambda w: pl.BlockSpec((bt, w), lambda i: (i, 0))
    st4 = pl.BlockSpec((bt, NH, HD, HD), lambda i: (i, 0, 0, 0))
    cvs = pl.BlockSpec((bt, GDN_CONV - 1, CONV_CH), lambda i: (i, 0, 0))
    full = lambda t: pl.BlockSpec(t.shape, lambda i: (0,) * t.ndim)
    srw, ssh, sret, sgdn, scv, shg = states
    return pl.pallas_call(
        _step_body,
        grid=(n // bt,),
        in_specs=[rows(D_MODEL), rows(D_MODEL), rows(D_MODEL), rows(LANES), rows(D_MODEL),
                  st4, rows(D_MODEL), st4, st4, cvs, st4] + [full(t) for t in params],
        out_specs=[rows(BR)] * 4 + [st4, rows(D_MODEL), st4, st4, cvs, st4],
        out_shape=[jax.ShapeDtypeStruct((n, BR), BF)] * 4
        + [jax.ShapeDtypeStruct(srw.shape, F32), jax.ShapeDtypeStruct((n, D_MODEL), F32),
           jax.ShapeDtypeStruct(sret.shape, F32), jax.ShapeDtypeStruct(sgdn.shape, F32),
           jax.ShapeDtypeStruct(scv.shape, F32), jax.ShapeDtypeStruct(shg.shape, F32)],
        compiler_params=pltpu.CompilerParams(dimension_semantics=("arbitrary",),
                                             vmem_limit_bytes=VMEM_LIMIT),
        name="sample_step",
    )(prw, pret, pgdn, pab, phg, srw, ssh, sret, sgdn, scv, shg, *params)


def _constants():
    c = CHUNK
    i = jnp.arange(c)
    tri = (i[:, None] >= i[None, :]).astype(BF)
    d = jnp.arange(BR)
    ones_bd = ((d[:, None] // HD) == (d[None, :] // HD)).astype(BF)
    return {"tri": tri, "triu": tri.T, "ones_bd": ones_bd, "eye": jnp.eye(HD, dtype=BF)}


def _layer_params(l, a):
    row = lambda t: t[l].reshape(1, -1).astype(F32)
    o_ret, o_gdn = D_MODEL, 2 * D_MODEL
    o_ab = o_gdn + 4 * BR
    o_hg = o_ab + 2 * NH
    o_gate = o_hg + 4 * BR
    w_in = a["w_in"][l]
    wab = jnp.pad(w_in[:, o_ab:o_hg], ((0, 0), (0, LANES - 2 * NH)))
    pad_row = lambda t: jnp.pad(t[l].reshape(1, -1).astype(F32), ((0, 0), (0, LANES - NH)))
    return {
        "norm_mix_w": row(a["norm_mix_w"]), "norm_ffn_w": row(a["norm_ffn_w"]),
        "w_mix": tuple(w.astype(BF) for w in (w_in[:, :o_ret], w_in[:, o_ret:o_gdn], w_in[:, o_gdn:o_ab], wab,
                                               w_in[:, o_hg:o_gate])),
        "w_gate": w_in[:, o_gate:].astype(BF), "w_branch": a["w_branch"][l].astype(BF),
        "w_out": a["w_out"][l].astype(BF), "w_gate_up": a["w_gate_up"][l].astype(BF),
        "w_down": a["w_down"][l].astype(BF),
        "rw_mu": row(a["rw_mu"]), "rw_w0": row(a["rw_w0"]), "rw_w2": a["rw_w2"][l].astype(BF),
        "rw_a0": row(a["rw_a0"]), "rw_a2": a["rw_a2"][l].astype(BF), "rw_g2": a["rw_g2"][l].astype(BF),
        "rw_k_k": row(a["rw_k_k"]), "rw_k_a": row(a["rw_k_a"]), "rw_r_k": row(a["rw_r_k"]),
        "rw_ln_w": row(a["rw_ln_w"]), "rw_ln_b": row(a["rw_ln_b"]),
        "gdn_conv_w": a["gdn_conv_w"][l].astype(F32), "gdn_a_log": pad_row(a["gdn_a_log"]),
        "gdn_dt_bias": pad_row(a["gdn_dt_bias"]),
        "gdn_norm_w": jnp.tile(a["gdn_norm_w"][l].reshape(1, HD).astype(F32), (1, NH)),
        "hg_norm_w": row(a["hg_norm_w"]),
    }


def kernel(x_prompt, x_sample, c_prompt, c_sample, state_rwkv, state_rwkv_shift, state_ret, state_gdn, state_gdn_conv, state_hgrn, ada_w, ada_b, norm_mix_w, w_in, rw_mu, rw_w0, rw_w2, rw_a0, rw_a2, rw_g2, rw_k_k, rw_k_a, rw_r_k, rw_ln_w, rw_ln_b, gdn_conv_w, gdn_a_log, gdn_dt_bias, gdn_norm_w, hg_lb_logits, hg_norm_w, w_branch, w_out, norm_ffn_w, w_gate_up, w_down, final_norm_w):
    a = dict(norm_mix_w=norm_mix_w, w_in=w_in, rw_mu=rw_mu, rw_w0=rw_w0, rw_w2=rw_w2, rw_a0=rw_a0, rw_a2=rw_a2,
             rw_g2=rw_g2, rw_k_k=rw_k_k, rw_k_a=rw_k_a, rw_r_k=rw_r_k, rw_ln_w=rw_ln_w, rw_ln_b=rw_ln_b,
             gdn_conv_w=gdn_conv_w, gdn_a_log=gdn_a_log, gdn_dt_bias=gdn_dt_bias, gdn_norm_w=gdn_norm_w,
             hg_norm_w=hg_norm_w, w_branch=w_branch, w_out=w_out, norm_ffn_w=norm_ffn_w, w_gate_up=w_gate_up,
             w_down=w_down)
    depth = ada_w.shape[0]
    Bp, Tp, _ = x_prompt.shape
    Bs = x_sample.shape[0]
    assert x_sample.shape[1] == 1 and Tp % CHUNK == 0
    tm = 256 if Tp % 256 == 0 else CHUNK
    consts = _constants()
    mod = _ada_mod(jnp.concatenate([c_prompt, c_sample], axis=0), ada_w.astype(F32), ada_b.astype(F32))
    lbs = _lower_bounds(hg_lb_logits)
    fw = final_norm_w.reshape(1, D_MODEL).astype(F32)

    xp = x_prompt
    xs = x_sample.reshape(1, Bs, D_MODEL)
    new_p = [[] for _ in range(6)]
    new_s = [[] for _ in range(6)]
    for l in range(depth):
        lp = _layer_params(l, a)
        lb = lbs[l:l + 1]
        mod_p = mod[l, :Bp].reshape(Bp, 1, -1)
        mod_s = mod[l, Bp:].reshape(1, Bs, -1)
        final = l == depth - 1

        h, prw, pret, pgdn, pab, phg = _in_proj(xp, mod_p, 1, tm, lp["norm_mix_w"], lp["w_mix"])
        y_rw, s_rw, s_sh = _rw_prompt(prw, lp, consts)
        y_ret, s_ret = _ret_prompt(pret, consts)
        y_gdn, s_gdn, s_cv = _gdn_prompt(pgdn, pab, lp["gdn_conv_w"], lp["gdn_a_log"], lp["gdn_dt_bias"],
                                         lp["gdn_norm_w"], consts)
        y_hg, s_hg = _hg_prompt(phg, lb, lp["hg_norm_w"], consts)
        xp = _merge(xp, h, (y_rw, y_ret, y_gdn, y_hg), mod_p, 1, tm, lp["w_gate"], lp["w_branch"], lp["w_out"])
        xp = _ffn(xp, mod_p, 1, tm, lp["norm_ffn_w"], lp["w_gate_up"], lp["w_down"], fw, final)
        for lst, s in zip(new_p, (s_rw, s_sh, s_ret, s_gdn, s_cv, s_hg)):
            lst.append(s)

        h, prw, pret, pgdn, pab, phg = _in_proj(xs, mod_s, Bs, Bs, lp["norm_mix_w"], lp["w_mix"])
        states = (state_rwkv[l], state_rwkv_shift[l].reshape(Bs, -1), state_ret[l], state_gdn[l],
                  state_gdn_conv[l], state_hgrn[l])
        outs = _step_sample(tuple(t[0] for t in (prw, pret, pgdn, pab, phg)), states, lp, lb, consts)
        ys = tuple(t[None] for t in outs[:4])
        xs = _merge(xs, h, ys, mod_s, Bs, Bs, lp["w_gate"], lp["w_branch"], lp["w_out"])
        xs = _ffn(xs, mod_s, Bs, Bs, lp["norm_ffn_w"], lp["w_gate_up"], lp["w_down"], fw, final)
        n_rw, n_sh, n_ret, n_gdn, n_cv, n_hg = outs[4:]
        for lst, s in zip(new_s, (n_rw, n_sh.reshape(Bs, 1, -1), n_ret, n_gdn, n_cv, n_hg)):
            lst.append(s)

    y_prompt = xp
    y_sample = xs.reshape(Bs, 1, D_MODEL)
    return (y_prompt, y_sample) + tuple(jnp.stack(t) for t in new_p) + tuple(jnp.stack(t) for t in new_s)
```

```python
import functools
import math

import jax
import jax.numpy as jnp
from jax import lax
from jax.experimental import pallas as pl
from jax.experimental.pallas import tpu as pltpu

F32 = jnp.float32
BF = jnp.bfloat16

D_MODEL = 1024
DEPTH = 4
PAST_LEN = 16384
BR = 256
HD = 64
NH = 4
RW_LORA_W = 64
RW_LORA_A = 64
RW_GN_EPS = 64e-5
ROPE_BASE = 10000.0
GDN_CONV = 4
CONV_CH = 3 * BR
D_FF = 2816
NORM_EPS = 1e-6
CHUNK = 64
SUB = 16
LANES = 128
VMEM_LIMIT = 56 * 1024 * 1024

NN = ((1,), (0,))
NT = ((1,), (1,))
TN = ((0,), (0,))


def _mm(a, b, dims=NN):
    return lax.dot_general(a.astype(BF), b.astype(BF), (dims, ((), ())), preferred_element_type=F32)


def _split2(a):
    hi = a.astype(BF)
    lo = (a - hi.astype(F32)).astype(BF)
    return hi, lo


def _split3(a):
    hi = a.astype(BF)
    r = a - hi.astype(F32)
    mid = r.astype(BF)
    lo = (r - mid.astype(F32)).astype(BF)
    return hi, mid, lo


def _mm_exact_rhs(a, b, dims=NN, parts=3):
    ps = _split3(a) if parts == 3 else _split2(a)
    out = _mm(ps[0], b, dims)
    for p in ps[1:]:
        out = out + _mm(p, b, dims)
    return out


def _mm_exact_lhs(a, b, dims=NN, parts=3):
    ps = _split3(b) if parts == 3 else _split2(b)
    out = _mm(a, ps[0], dims)
    for p in ps[1:]:
        out = out + _mm(a, p, dims)
    return out


def _sigmoid(x):
    return jax.nn.sigmoid(x)


def _silu(x):
    return x * jax.nn.sigmoid(x)


def _softplus(x):
    return jnp.maximum(x, 0.0) + jnp.log1p(jnp.exp(-jnp.abs(x)))


def _head_sum(x, ones_bd):
    return _mm(x, ones_bd)


def _tri_inverse(Ls, row, col):
    c = Ls[0].shape[0]
    eye = (row == col).astype(F32)
    bd = (row // SUB) == (col // SUB)
    Ps = [jnp.where(bd, L, 0.0) for L in Ls]
    Ds = [eye - P for P in Ps]
    for _ in range(int(math.log2(SUB)) - 1):
        Ps = [_mm(P, P) for P in Ps]
        Ds = [Dm + _mm(Dm, P) for Dm, P in zip(Ds, Ps)]
    size = SUB
    while size < c:
        off = ((row // (2 * size)) == (col // (2 * size))) & ((row // size) != (col // size))
        DCs = [_mm(Dm, jnp.where(off, L, 0.0)) for Dm, L in zip(Ds, Ls)]
        Ds = [Dm - _mm(DC, Dm) for Dm, DC in zip(Ds, DCs)]
        size *= 2
    return Ds


def _mod_body(c_ref, w_ref, b_ref, o_ref):
    o_ref[0] = _mm(_silu(c_ref[...]), w_ref[0]) + b_ref[0]


def _ada_mod(c_all, ada_w, ada_b):
    n = c_all.shape[0]
    ncol = ada_w.shape[2] // D_MODEL
    return pl.pallas_call(
        _mod_body,
        grid=(DEPTH, ncol),
        in_specs=[
            pl.BlockSpec((n, D_MODEL), lambda l, j: (0, 0)),
            pl.BlockSpec((1, D_MODEL, D_MODEL), lambda l, j: (l, 0, j)),
            pl.BlockSpec((1, 1, D_MODEL), lambda l, j: (l, 0, j)),
        ],
        out_specs=pl.BlockSpec((1, n, D_MODEL), lambda l, j: (l, 0, j)),
        out_shape=jax.ShapeDtypeStruct((DEPTH, n, ncol * D_MODEL), F32),
        compiler_params=pltpu.CompilerParams(dimension_semantics=("arbitrary", "arbitrary"),
                                             vmem_limit_bytes=VMEM_LIMIT),
        name="ada_mod",
    )(c_all, ada_w, ada_b.reshape(DEPTH, 1, -1))


def _lb_body(x_ref, o_ref):
    x = x_ref[...]
    m = jnp.max(x, axis=0, keepdims=True)
    e = jnp.exp(x - m)
    p = e / jnp.sum(e, axis=0, keepdims=True)
    acc = jnp.zeros_like(p[0:1])
    rows = [acc]
    for l in range(1, DEPTH):
        acc = acc + p[l:l + 1]
        rows.append(acc)
    o_ref[...] = jnp.concatenate(rows, axis=0)


def _lower_bounds(logits):
    return pl.pallas_call(
        _lb_body,
        out_shape=jax.ShapeDtypeStruct(logits.shape, F32),
        name="hgrn_lower_bounds",
    )(logits.astype(F32))


def _mod_spec(rm, tm, col):
    if rm == 1:
        return pl.BlockSpec((1, 1, D_MODEL), lambda b, i: (b, 0, col))
    return pl.BlockSpec((1, tm, D_MODEL), lambda b, i: (b, i, col))


def _full_spec(shape):
    nd = len(shape)
    return pl.BlockSpec(shape, lambda b, i: (0,) * nd)


def _row_spec(tm, width):
    return pl.BlockSpec((1, tm, width), lambda b, i: (b, i, 0))


def _modulated_norm(x, nw, sc, sh):
    ms = jnp.mean(x * x, axis=-1, keepdims=True)
    return (x * lax.rsqrt(ms + NORM_EPS) * nw) * (1.0 + sc) + sh


def _in_body(x_ref, sc_ref, sh_ref, nw_ref, wrw_ref, wret_ref, wgdn_ref, wab_ref, whg_ref,
             h_ref, prw_ref, pret_ref, pgdn_ref, pab_ref, phg_ref):
    h = _modulated_norm(x_ref[0], nw_ref[...], sc_ref[0], sh_ref[0]).astype(BF)
    h_ref[0] = h
    prw_ref[0] = _mm(h, wrw_ref[...])
    pret_ref[0] = _mm(h, wret_ref[...])
    pgdn_ref[0] = _mm(h, wgdn_ref[...])
    pab_ref[0] = _mm(h, wab_ref[...])
    phg_ref[0] = _mm(h, whg_ref[...])


def _in_proj(x, mod, rm, tm, nw, ws):
    G, R, _ = x.shape
    wrw, wret, wgdn, wab, whg = ws
    outs = [jax.ShapeDtypeStruct((G, R, D_MODEL), BF)] + [
        jax.ShapeDtypeStruct((G, R, w.shape[1]), F32) for w in ws]
    return pl.pallas_call(
        _in_body,
        grid=(G, R // tm),
        in_specs=[_row_spec(tm, D_MODEL), _mod_spec(rm, tm, 1), _mod_spec(rm, tm, 0),
                  _full_spec((1, D_MODEL))] + [_full_spec(w.shape) for w in ws],
        out_specs=[_row_spec(tm, D_MODEL)] + [_row_spec(tm, w.shape[1]) for w in ws],
        out_shape=outs,
        compiler_params=pltpu.CompilerParams(dimension_semantics=("arbitrary", "arbitrary"),
                                             vmem_limit_bytes=VMEM_LIMIT),
        name="in_proj",
    )(x, mod, mod, nw, wrw, wret, wgdn, wab, whg)


def _merge_body(x_ref, h_ref, yrw_ref, yret_ref, ygdn_ref, yhg_ref, g1_ref, wg_ref, wb_ref, wo_ref,
                o_ref):
    h = h_ref[0]
    merged = None
    for n, y_ref in enumerate((yrw_ref, yret_ref, ygdn_ref, yhg_ref)):
        gate = _sigmoid(_mm(h, wg_ref[:, n * D_MODEL:(n + 1) * D_MODEL]))
        term = gate * _mm(y_ref[0], wb_ref[n])
        merged = term if merged is None else merged + term
    o_ref[0] = x_ref[0] + g1_ref[0] * _mm(merged, wo_ref[...])


def _merge(x, h, ys, mod, rm, tm, wg, wb, wo):
    G, R, _ = x.shape
    return pl.pallas_call(
        _merge_body,
        grid=(G, R // tm),
        in_specs=[_row_spec(tm, D_MODEL), _row_spec(tm, D_MODEL)] + [_row_spec(tm, BR)] * 4
        + [_mod_spec(rm, tm, 2), _full_spec(wg.shape), _full_spec(wb.shape), _full_spec(wo.shape)],
        out_specs=_row_spec(tm, D_MODEL),
        out_shape=jax.ShapeDtypeStruct((G, R, D_MODEL), F32),
        compiler_params=pltpu.CompilerParams(dimension_semantics=("arbitrary", "arbitrary"),
                                             vmem_limit_bytes=VMEM_LIMIT),
        name="merge_out",
    )(x, h, *ys, mod, wg, wb, wo)


def _ffn_body(final, x_ref, sc_ref, sh_ref, g2_ref, nw_ref, wgu_ref, wd_ref, fw_ref, o_ref):
    x = x_ref[0]
    h = _modulated_norm(x, nw_ref[...], sc_ref[0], sh_ref[0]).astype(BF)
    gate = _mm(h, wgu_ref[:, :D_FF])
    up = _mm(h, wgu_ref[:, D_FF:])
    x = x + g2_ref[0] * _mm(_silu(gate) * up, wd_ref[...])
    if final:
        ms = jnp.mean(x * x, axis=-1, keepdims=True)
        x = x * lax.rsqrt(ms + NORM_EPS) * fw_ref[...]
    o_ref[0] = x


def _ffn(x, mod, rm, tm, nw, wgu, wd, fw, final):
    G, R, _ = x.shape
    return pl.pallas_call(
        functools.partial(_ffn_body, final),
        grid=(G, R // tm),
        in_specs=[_row_spec(tm, D_MODEL), _mod_spec(rm, tm, 4), _mod_spec(rm, tm, 3), _mod_spec(rm, tm, 5),
                  _full_spec((1, D_MODEL)), _full_spec(wgu.shape), _full_spec(wd.shape),
                  _full_spec((1, D_MODEL))],
        out_specs=_row_spec(tm, D_MODEL),
        out_shape=jax.ShapeDtypeStruct((G, R, D_MODEL), F32),
        compiler_params=pltpu.CompilerParams(dimension_semantics=("arbitrary", "arbitrary"),
                                             vmem_limit_bytes=VMEM_LIMIT),
        name="ffn",
    )(x, mod, mod, mod, nw, wgu, wd, fw)


def _chunk_spec(c, width):
    return pl.BlockSpec((1, c, width), lambda b, i: (b, i, 0))


def _state_spec():
    return pl.BlockSpec((1, NH, HD, HD), lambda b, i: (b, 0, 0, 0))


def _iota2(shape):
    return (lax.broadcasted_iota(jnp.int32, shape, 0), lax.broadcasted_iota(jnp.int32, shape, 1))


def _rope(t, cosf, sins):
    lane = lax.broadcasted_iota(jnp.int32, t.shape, 1)
    width = t.shape[1]
    rot = jnp.where((lane % HD) < HD // 2, pltpu.roll(t, width - HD // 2, 1), pltpu.roll(t, HD // 2, 1))
    return t * cosf + rot * sins


def _ret_body(p_ref, cos_ref, sin_ref, qin_ref, kout_ref, dmat_ref, cdec_ref, ones_ref, y_ref, s_ref):
    @pl.when(pl.program_id(1) == 0)
    def _():
        s_ref[...] = jnp.zeros_like(s_ref)

    P = p_ref[0]
    cosf, sins = cos_ref[...], sin_ref[...]
    q = _rope(P[:, :BR], cosf, sins)
    k = _rope(P[:, BR:2 * BR], cosf, sins) * HD ** -0.5
    v = P[:, 2 * BR:3 * BR]
    g = P[:, 3 * BR:]
    qs = q * qin_ref[...]
    ko = k * kout_ref[...]
    outs = []
    for h in range(NH):
        sl = slice(h * HD, (h + 1) * HD)
        S = s_ref[0, h]
        s = _mm(q[:, sl], k[:, sl], NT) * dmat_ref[h]
        outs.append(_mm(s, v[:, sl]) + _mm(qs[:, sl], S))
        s_ref[0, h] = S * cdec_ref[h] + _mm(ko[:, sl], v[:, sl], TN)
    o = jnp.concatenate(outs, axis=1)
    ms = _head_sum(o * o, ones_ref[...]) * (1.0 / HD)
    y_ref[0] = (o * lax.rsqrt(ms + NORM_EPS) * _silu(g)).astype(BF)


def _ret_tables(c):
    lg = jnp.log1p(-jnp.exp2(-5.0 - jnp.arange(NH, dtype=F32)))[:, None]
    idx = jnp.arange(c, dtype=F32)
    diff = idx[:, None] - idx[None, :]
    dmat = jnp.where(diff >= 0, jnp.exp(lg[:, :, None] * jnp.maximum(diff, 0.0)), 0.0)
    q_in = jnp.exp(lg * (idx + 1.0))
    k_out = jnp.exp(lg * (c - 1.0 - idx))
    c_dec = jnp.exp(lg * c)
    expand = lambda t: jnp.repeat(t.T, HD, axis=1)
    return dmat, expand(q_in), expand(k_out), jnp.broadcast_to(c_dec[:, :, None], (NH, HD, HD))


def _rope_tables(pos):
    half = HD // 2
    freqs = ROPE_BASE ** (-jnp.arange(half, dtype=F32) / half)
    ang = pos.astype(F32)[:, None] * freqs
    cos, sin = jnp.cos(ang), jnp.sin(ang)
    cosf = jnp.tile(jnp.concatenate([cos, cos], axis=1), (1, NH))
    sins = jnp.tile(jnp.concatenate([-sin, sin], axis=1), (1, NH))
    return cosf, sins


def _ret_prompt(p, consts):
    B, T, _ = p.shape
    c = CHUNK
    cosf, sins = _rope_tables(jnp.arange(T))
    dmat, q_in, k_out, c_dec = _ret_tables(c)
    return pl.pallas_call(
        _ret_body,
        grid=(B, T // c),
        in_specs=[_chunk_spec(c, D_MODEL),
                  pl.BlockSpec((c, BR), lambda b, i: (i, 0)), pl.BlockSpec((c, BR), lambda b, i: (i, 0)),
                  _full_spec((c, BR)), _full_spec((c, BR)), _full_spec((NH, c, c)), _full_spec((NH, HD, HD)),
                  _full_spec((BR, BR))],
        out_specs=[_chunk_spec(c, BR), _state_spec()],
        out_shape=[jax.ShapeDtypeStruct((B, T, BR), BF), jax.ShapeDtypeStruct((B, NH, HD, HD), F32)],
        compiler_params=pltpu.CompilerParams(dimension_semantics=("arbitrary", "arbitrary"),
                                             vmem_limit_bytes=VMEM_LIMIT),
        name="ret_prompt",
    )(p, cosf, sins, q_in, k_out, dmat, c_dec, consts["ones_bd"])


def _chunks_per_step(T):
    for nc in (4, 2):
        if T % (nc * CHUNK) == 0:
            return nc
    return 1


def _gdn_body(nc, pg_ref, pab_ref, cw_ref, alog_ref, dtb_ref, nw_ref, tri_ref, triu_ref, ones_ref,
              y_ref, s_ref, cv_ref, xext):
    c = CHUNK
    tb = nc * c

    @pl.when(pl.program_id(1) == 0)
    def _():
        s_ref[...] = jnp.zeros_like(s_ref)
        xext[pl.ds(0, 8), :] = jnp.zeros((8, CONV_CH), F32)

    P = pg_ref[0]
    X = P[:, :CONV_CH]
    xext[pl.ds(8, tb), :] = X
    cw = cw_ref[...]
    conv = xext[pl.ds(8 - 3, tb), :] * cw[0:1]
    conv = conv + xext[pl.ds(8 - 2, tb), :] * cw[1:2]
    conv = conv + xext[pl.ds(8 - 1, tb), :] * cw[2:3]
    conv = conv + X * cw[3:4]
    xext[pl.ds(0, 8), :] = X[tb - 8:tb]
    cv_ref[0] = X[tb - (GDN_CONV - 1):tb]
    conv = _silu(conv)
    ones_bd = ones_ref[...]
    qn, kn, v = conv[:, :BR], conv[:, BR:2 * BR], conv[:, 2 * BR:]
    q = qn * lax.rsqrt(_head_sum(qn * qn, ones_bd) + 1e-6) * HD ** -0.5
    k = kn * lax.rsqrt(_head_sum(kn * kn, ones_bd) + 1e-6)
    gate = P[:, CONV_CH:]

    ab = pab_ref[0]
    glog = -jnp.exp(alog_ref[...]) * _softplus(ab + dtb_ref[...])
    beta = _sigmoid(ab)
    Gc = _mm_exact_lhs(tri_ref[...], glog, NN)
    Gr = _mm_exact_rhs(glog, triu_ref[...], TN)
    eG = jnp.exp(Gc)
    row, col = _iota2((c, c))
    incl = row >= col
    strict = row > col
    items = [(ci, h) for ci in range(nc) for h in range(NH)]

    def sl(t, ci, h):
        return t[ci * c:(ci + 1) * c, h * HD:(h + 1) * HD]

    KQ = [_mm(jnp.concatenate([sl(k, ci, h), sl(q, ci, h)], axis=0), sl(k, ci, h), NT) for ci, h in items]
    Ls, qks, rhss, kouts, cdecs = [], [], [], [], []
    for (ci, h), kq in zip(items, KQ):
        rows = slice(ci * c, (ci + 1) * c)
        gc = Gc[rows, h:h + 1]
        gr = Gr[h:h + 1, rows]
        bcol = beta[rows, NH + h:NH + h + 1]
        gam = jnp.where(incl, jnp.exp(jnp.where(incl, gc - gr, 0.0)), 0.0)
        Ls.append(jnp.where(strict, bcol * gam * kq[:c], 0.0))
        qks.append(kq[c:] * gam)
        rhss.append(jnp.concatenate([sl(v, ci, h) * bcol, sl(k, ci, h) * (bcol * eG[rows, h:h + 1])], axis=1))
        gend = gc[c - 1:c, :]
        kouts.append(sl(k, ci, h) * jnp.exp(gend - gc))
        cdecs.append(jnp.exp(gend))
    Ts = _tri_inverse(Ls, row, col)
    sols = [_mm(T, rhs) for T, rhs in zip(Ts, rhss)]
    QS = [_mm(qk, sol) for qk, sol in zip(qks, sols)]
    KS = [_mm(kout, sol, TN) for kout, sol in zip(kouts, sols)]
    outs = [[None] * NH for _ in range(nc)]
    for idx, (ci, h) in enumerate(items):
        rows = slice(ci * c, (ci + 1) * c)
        QW = sl(q, ci, h) * eG[rows, h:h + 1] - QS[idx][:, HD:]
        S = s_ref[0, h]
        Z = _mm(jnp.concatenate([KS[idx][:, HD:], QW], axis=0), S)
        s_ref[0, h] = S * cdecs[idx] - Z[:HD] + KS[idx][:, :HD]
        outs[ci][h] = Z[HD:] + QS[idx][:, :HD]
    o = jnp.concatenate([jnp.concatenate(r, axis=1) for r in outs], axis=0)
    ms = _head_sum(o * o, ones_bd) * (1.0 / HD)
    y_ref[0] = (o * lax.rsqrt(ms + NORM_EPS) * nw_ref[...] * _silu(gate)).astype(BF)


def _gdn_prompt(pg, pab, cw, alog, dtb, nw, consts):
    B, T, _ = pg.shape
    nc = _chunks_per_step(T)
    tb = nc * CHUNK
    return pl.pallas_call(
        functools.partial(_gdn_body, nc),
        grid=(B, T // tb),
        in_specs=[_chunk_spec(tb, D_MODEL), _chunk_spec(tb, LANES), _full_spec((GDN_CONV, CONV_CH)),
                  _full_spec((1, LANES)), _full_spec((1, LANES)), _full_spec((1, BR)),
                  _full_spec((tb, tb)), _full_spec((tb, tb)), _full_spec((BR, BR))],
        out_specs=[_chunk_spec(tb, BR), _state_spec(),
                   pl.BlockSpec((1, GDN_CONV - 1, CONV_CH), lambda b, i: (b, 0, 0))],
        out_shape=[jax.ShapeDtypeStruct((B, T, BR), BF), jax.ShapeDtypeStruct((B, NH, HD, HD), F32),
                   jax.ShapeDtypeStruct((B, GDN_CONV - 1, CONV_CH), F32)],
        scratch_shapes=[pltpu.VMEM((tb + 8, CONV_CH), F32)],
        compiler_params=pltpu.CompilerParams(dimension_semantics=("arbitrary", "arbitrary"),
                                             vmem_limit_bytes=VMEM_LIMIT),
        name="gdn_prompt",
    )(pg, pab, cw, alog, dtb, nw, consts["tri%d" % nc], consts["triu%d" % nc], consts["ones_bd"])


def _hgrn_gates(xq, xf, lb):
    q = _silu(xq)
    ls = jnp.minimum(xf, 0.0) - jnp.log1p(jnp.exp(-jnp.abs(xf)))
    pos = lb > 0.0
    a = jnp.log(jnp.where(pos, lb, 1.0))
    b = jnp.log1p(-lb) + ls
    lae = jnp.maximum(a, b) + jnp.log1p(jnp.exp(-jnp.abs(a - b)))
    log_f = jnp.where(pos, lae, ls)
    k = (1.0 - lb) * _sigmoid(-xf)
    return q, log_f, k


def _hg_offsets(c):
    offs, total = [], 0
    for j in range(c):
        offs.append(total)
        total += c - (j // 8) * 8
    return offs, total


def _hg_body(p_ref, lb_ref, nw_ref, tri_ref, ones_ref, eye_ref, y_ref, s_ref, st, tbuf, sbuf):
    c = CHUNK
    ci = pl.program_id(1)

    @pl.when(ci == 0)
    def _():
        st[...] = jnp.zeros_like(st)

    P = p_ref[0]
    q, log_f, k = _hgrn_gates(P[:, :BR], P[:, BR:2 * BR], lb_ref[...])
    v = P[:, 2 * BR:3 * BR]
    og = P[:, 3 * BR:]
    G = _mm_exact_lhs(tri_ref[...], log_f, NN)
    offs, total = _hg_offsets(c)
    rowi = lax.broadcasted_iota(jnp.int32, (8, BR), 0)
    for j in range(c):
        i0 = (j // 8) * 8
        d = G[i0:, :] - G[j:j + 1, :]
        e = jnp.exp(jnp.minimum(d, 0.0))
        t = q[i0:, :] * e * k[j:j + 1, :]
        if j % 8:
            head = jnp.where(rowi >= (j - i0), t[0:8], 0.0)
            t = jnp.concatenate([head, t[8:]], axis=0) if c - i0 > 8 else head
        tbuf[pl.ds(offs[j], c - i0), :] = t
    sbuf[...] = _head_sum(tbuf[...], ones_ref[...])
    blocks = [jnp.zeros((8, BR), F32) for _ in range(c // 8)]
    for j in range(c):
        i0 = (j // 8) * 8
        contrib = sbuf[pl.ds(offs[j], c - i0), :] * v[j:j + 1, :]
        for ib in range(i0 // 8, c // 8):
            blocks[ib] = blocks[ib] + contrib[ib * 8 - i0:ib * 8 - i0 + 8]
    o = jnp.concatenate(blocks, axis=0)
    qg = q * jnp.exp(G)
    gend = G[c - 1:c, :]
    kd = k * jnp.exp(gend - G)
    dec = jnp.exp(gend)
    outs = []
    for h in range(NH):
        sl = slice(h * HD, (h + 1) * HD)
        ST = st[h]
        outs.append(_mm(qg[:, sl], ST, NT))
        st[h] = ST * dec[:, sl] + _mm(v[:, sl], kd[:, sl], TN)
    o = o + jnp.concatenate(outs, axis=1)
    ms = _head_sum(o * o, ones_ref[...]) * (1.0 / HD)
    y_ref[0] = (o * lax.rsqrt(ms + NORM_EPS) * nw_ref[...] * _sigmoid(og)).astype(BF)

    @pl.when(ci == pl.num_programs(1) - 1)
    def _():
        for h in range(NH):
            s_ref[0, h] = _mm_exact_lhs(eye_ref[...], st[h], NT)


def _hg_prompt(p, lb, nw, consts):
    B, T, _ = p.shape
    c = CHUNK
    _, total = _hg_offsets(c)
    return pl.pallas_call(
        _hg_body,
        grid=(B, T // c),
        in_specs=[_chunk_spec(c, D_MODEL), _full_spec((1, BR)), _full_spec((1, BR)), _full_spec((c, c)),
                  _full_spec((BR, BR)), _full_spec((HD, HD))],
        out_specs=[_chunk_spec(c, BR), _state_spec()],
        out_shape=[jax.ShapeDtypeStruct((B, T, BR), BF), jax.ShapeDtypeStruct((B, NH, HD, HD), F32)],
        scratch_shapes=[pltpu.VMEM((NH, HD, HD), F32), pltpu.VMEM((total, BR), F32),
                        pltpu.VMEM((total, BR), F32)],
        compiler_params=pltpu.CompilerParams(dimension_semantics=("arbitrary", "arbitrary"),
                                             vmem_limit_bytes=VMEM_LIMIT),
        name="hgrn_prompt",
    )(p, lb, nw, consts["tri"], consts["ones_bd"], consts["eye"])


def _rw_features(P, prev, mu, w0, w2, a0, a2, g2, kkw, kaw, ones_bd):
    Pm = P + mu * (prev - P)
    r, kx, vx = Pm[:, :BR], Pm[:, BR:2 * BR], Pm[:, 2 * BR:3 * BR]
    o4 = 3 * BR + RW_LORA_W
    o5 = o4 + RW_LORA_A
    wl, al, gl = Pm[:, 3 * BR:o4], Pm[:, o4:o5], Pm[:, o5:]
    zw = w0 + _mm(jnp.tanh(wl), w2)
    lw = -jnp.exp(-_softplus(-zw) - 0.5)
    a = _sigmoid(a0 + _mm(al, a2))
    g = _mm(_sigmoid(gl), g2)
    kks = kx * kkw
    kk = kks * lax.rsqrt(_head_sum(kks * kks, ones_bd) + 1e-6)
    k = kx * (1.0 + (a - 1.0) * kaw)
    return r, lw, k, vx, -kk, kk * a, g


def _rw_output(o, r, k, vx, g, rkw, lnw, lnb, ones_bd):
    mean = _head_sum(o, ones_bd) * (1.0 / HD)
    cen = o - mean
    var = _head_sum(cen * cen, ones_bd) * (1.0 / HD)
    on = cen * lax.rsqrt(var + RW_GN_EPS) * lnw + lnb
    bonus = _head_sum(r * k * rkw, ones_bd) * vx
    return (on + bonus) * g


def _rw_body(nc, p_ref, mu_ref, w0_ref, w2_ref, a0_ref, a2_ref, g2_ref, kkw_ref, kaw_ref, rkw_ref,
             lnw_ref, lnb_ref, tri_ref, ones_ref, eye_ref, y_ref, s_ref, sh_ref, st, prev):
    c = CHUNK
    tb = nc * c
    gi = pl.program_id(1)

    @pl.when(gi == 0)
    def _():
        st[...] = jnp.zeros_like(st)
        prev[...] = jnp.zeros_like(prev)

    P = p_ref[0]
    rowp = lax.broadcasted_iota(jnp.int32, P.shape, 0)
    shifted = jnp.where(rowp == 0, prev[0:1, :], pltpu.roll(P, 1, 0))
    prev[0:1, :] = P[tb - 1:tb, :]
    sh_ref[0] = P[tb - 1:tb, :]
    ones_bd = ones_ref[...]
    r, lw, k, vx, av, bv, g = _rw_features(P, shifted, mu_ref[...], w0_ref[...], w2_ref[...], a0_ref[...],
                                           a2_ref[...], g2_ref[...], kkw_ref[...], kaw_ref[...], ones_bd)
    LG = _mm_exact_lhs(tri_ref[...], lw, NN)
    LGE = jnp.concatenate([jnp.broadcast_to(LG[(ci + 1) * c - 1:(ci + 1) * c], (c, BR)) for ci in range(nc)],
                          axis=0)
    rt = r * jnp.exp(LG)
    at = av * jnp.exp(LG - lw)
    einv = jnp.exp(-LG)
    kh_ = k * einv
    bh_ = bv * einv
    eo = jnp.exp(LGE - LG)
    kout = k * eo
    bout = bv * eo
    row, col = _iota2((c, c))
    incl = row >= col
    strict = row > col
    items = [(ci, h) for ci in range(nc) for h in range(NH)]

    def sl(t, ci, h):
        return t[ci * c:(ci + 1) * c, h * HD:(h + 1) * HD]

    Ms = [_mm(jnp.concatenate([sl(at, ci, h), sl(rt, ci, h)], axis=0),
              jnp.concatenate([sl(bh_, ci, h), sl(kh_, ci, h)], axis=0), NT) for ci, h in items]
    Aabs = [jnp.where(strict, M[:c, :c], 0.0) for M in Ms]
    Arbs = [jnp.where(incl, M[c:, :c], 0.0) for M in Ms]
    AKs = [jnp.concatenate([jnp.where(strict, M[:c, c:], 0.0), jnp.where(incl, M[c:, c:], 0.0)], axis=0)
           for M in Ms]
    AVs = [_mm(AK, sl(vx, ci, h)) for AK, (ci, h) in zip(AKs, items)]
    Ts = _tri_inverse([-A for A in Aabs], row, col)
    WTs = [_mm(T, jnp.concatenate([AV[:c], sl(at, ci, h)], axis=1))
           for T, AV, (ci, h) in zip(Ts, AVs, items)]
    AWs = [_mm(Arb, WT) for Arb, WT in zip(Arbs, WTs)]
    CBs = [_mm(WT, sl(bout, ci, h), TN) for WT, (ci, h) in zip(WTs, items)]
    VKs = [_mm(sl(vx, ci, h), sl(kout, ci, h), TN) for ci, h in items]
    outs = [[None] * NH for _ in range(nc)]
    for idx, (ci, h) in enumerate(items):
        dec = jnp.exp(LG[(ci + 1) * c - 1:(ci + 1) * c, h * HD:(h + 1) * HD])
        RQ = sl(rt, ci, h) + AWs[idx][:, HD:]
        ST = st[h]
        outs[ci][h] = _mm(RQ, ST, NT) + AWs[idx][:, :HD] + AVs[idx][c:]
        st[h] = ST * dec + _mm(ST, CBs[idx][HD:]) + CBs[idx][:HD] + VKs[idx]
    o = jnp.concatenate([jnp.concatenate(rw, axis=1) for rw in outs], axis=0)
    y_ref[0] = _rw_output(o, r, k, vx, g, rkw_ref[...], lnw_ref[...], lnb_ref[...], ones_bd).astype(BF)

    @pl.when(gi == pl.num_programs(1) - 1)
    def _():
        for h in range(NH):
            s_ref[0, h] = _mm_exact_lhs(eye_ref[...], st[h], NT)


def _rw_prompt(p, lp, consts):
    B, T, _ = p.shape
    nc = _chunks_per_step(T)
    tb = nc * CHUNK
    params = [lp[n] for n in ("rw_mu", "rw_w0", "rw_w2", "rw_a0", "rw_a2", "rw_g2", "rw_k_k", "rw_k_a",
                              "rw_r_k", "rw_ln_w", "rw_ln_b")]
    return pl.pallas_call(
        functools.partial(_rw_body, nc),
        grid=(B, T // tb),
        in_specs=[_chunk_spec(tb, D_MODEL)] + [_full_spec(t.shape) for t in params]
        + [_full_spec((tb, tb)), _full_spec((BR, BR)), _full_spec((HD, HD))],
        out_specs=[_chunk_spec(tb, BR), _state_spec(),
                   pl.BlockSpec((1, 1, D_MODEL), lambda b, i: (b, 0, 0))],
        out_shape=[jax.ShapeDtypeStruct((B, T, BR), BF), jax.ShapeDtypeStruct((B, NH, HD, HD), F32),
                   jax.ShapeDtypeStruct((B, 1, D_MODEL), F32)],
        scratch_shapes=[pltpu.VMEM((NH, HD, HD), F32), pltpu.VMEM((8, D_MODEL), F32)],
        compiler_params=pltpu.CompilerParams(dimension_semantics=("arbitrary", "arbitrary"),
                                             vmem_limit_bytes=VMEM_LIMIT),
        name="rwkv_prompt",
    )(p, *params, consts["tri%d" % nc], consts["ones_bd"], consts["eye"])


def _col(x):
    return x[:, :, None]


def _rowv(x):
    return x[:, None, :]


def _step_body(prw_ref, pret_ref, pgdn_ref, pab_ref, phg_ref,
               srw_ref, ssh_ref, sret_ref, sgdn_ref, scv_ref, shg_ref,
               mu_ref, w0_ref, w2_ref, a0_ref, a2_ref, g2_ref, kkw_ref, kaw_ref, rkw_ref, lnw_ref, lnb_ref,
               cos_ref, sin_ref, cw_ref, alog_ref, dtb_ref, gnw_ref, lb_ref, hnw_ref, ones_ref,
               yrw_ref, yret_ref, ygdn_ref, yhg_ref,
               nrw_ref, nsh_ref, nret_ref, ngdn_ref, ncv_ref, nhg_ref):
    ones_bd = ones_ref[...]

    P = prw_ref[...]
    nsh_ref[...] = P
    r, lw, k, vx, av, bv, g = _rw_features(P, ssh_ref[...], mu_ref[...], w0_ref[...], w2_ref[...], a0_ref[...],
                                           a2_ref[...], g2_ref[...], kkw_ref[...], kaw_ref[...], ones_bd)
    w = jnp.exp(lw)
    outs = []
    for h in range(NH):
        sl = slice(h * HD, (h + 1) * HD)
        S = srw_ref[:, h]
        sa = jnp.sum(S * _col(av[:, sl]), axis=1, keepdims=True)
        S = _col(w[:, sl]) * S + _col(bv[:, sl]) * sa + _col(k[:, sl]) * _rowv(vx[:, sl])
        nrw_ref[:, h] = S
        outs.append(jnp.sum(S * _col(r[:, sl]), axis=1))
    o = jnp.concatenate(outs, axis=1)
    yrw_ref[...] = _rw_output(o, r, k, vx, g, rkw_ref[...], lnw_ref[...], lnb_ref[...], ones_bd).astype(BF)

    P = pret_ref[...]
    cosf, sins = cos_ref[...], sin_ref[...]
    q = _rope(P[:, :BR], cosf, sins)
    k = _rope(P[:, BR:2 * BR], cosf, sins) * HD ** -0.5
    v = P[:, 2 * BR:3 * BR]
    outs = []
    for h in range(NH):
        sl = slice(h * HD, (h + 1) * HD)
        S = sret_ref[:, h] * (1.0 - 2.0 ** (-5 - h)) + _col(k[:, sl]) * _rowv(v[:, sl])
        nret_ref[:, h] = S
        outs.append(jnp.sum(S * _col(q[:, sl]), axis=1))
    o = jnp.concatenate(outs, axis=1)
    ms = _head_sum(o * o, ones_bd) * (1.0 / HD)
    yret_ref[...] = (o * lax.rsqrt(ms + NORM_EPS) * _silu(P[:, 3 * BR:])).astype(BF)

    P = pgdn_ref[...]
    X = P[:, :CONV_CH]
    cw = cw_ref[...]
    conv = scv_ref[:, 0, :] * cw[0:1] + scv_ref[:, 1, :] * cw[1:2] + scv_ref[:, 2, :] * cw[2:3] + X * cw[3:4]
    ncv_ref[:, 0, :] = scv_ref[:, 1, :]
    ncv_ref[:, 1, :] = scv_ref[:, 2, :]
    ncv_ref[:, 2, :] = X
    conv = _silu(conv)
    qn, kn, v = conv[:, :BR], conv[:, BR:2 * BR], conv[:, 2 * BR:]
    q = qn * lax.rsqrt(_head_sum(qn * qn, ones_bd) + 1e-6) * HD ** -0.5
    k = kn * lax.rsqrt(_head_sum(kn * kn, ones_bd) + 1e-6)
    ab = pab_ref[...]
    eg = jnp.exp(-jnp.exp(alog_ref[...]) * _softplus(ab + dtb_ref[...]))
    beta = _sigmoid(ab)
    outs = []
    for h in range(NH):
        sl = slice(h * HD, (h + 1) * HD)
        egh = eg[:, h:h + 1][:, :, None]
        bh = beta[:, NH + h:NH + h + 1][:, :, None]
        S = sgdn_ref[:, h]
        kS = jnp.sum(S * _col(k[:, sl]), axis=1, keepdims=True)
        u = bh * (_rowv(v[:, sl]) - egh * kS)
        S = egh * S + _col(k[:, sl]) * u
        ngdn_ref[:, h] = S
        outs.append(jnp.sum(S * _col(q[:, sl]), axis=1))
    o = jnp.concatenate(outs, axis=1)
    ms = _head_sum(o * o, ones_bd) * (1.0 / HD)
    ygdn_ref[...] = (o * lax.rsqrt(ms + NORM_EPS) * gnw_ref[...] * _silu(P[:, CONV_CH:])).astype(BF)

    P = phg_ref[...]
    q, log_f, k = _hgrn_gates(P[:, :BR], P[:, BR:2 * BR], lb_ref[...])
    f = jnp.exp(log_f)
    v = P[:, 2 * BR:3 * BR]
    outs = []
    for h in range(NH):
        sl = slice(h * HD, (h + 1) * HD)
        S = shg_ref[:, h] * _col(f[:, sl]) + _col(k[:, sl]) * _rowv(v[:, sl])
        nhg_ref[:, h] = S
        outs.append(jnp.sum(S * _col(q[:, sl]), axis=1))
    o = jnp.concatenate(outs, axis=1)
    ms = _head_sum(o * o, ones_bd) * (1.0 / HD)
    yhg_ref[...] = (o * lax.rsqrt(ms + NORM_EPS) * hnw_ref[...] * _sigmoid(P[:, 3 * BR:])).astype(BF)


def _step_sample(ps, states, lp, lb, consts, bt=8):
    prw, pret, pgdn, pab, phg = ps
    n = prw.shape[0]
    cosf, sins = _rope_tables(PAST_LEN + jnp.arange(1))
    params = [lp[nm] for nm in ("rw_mu", "rw_w0", "rw_w2", "rw_a0", "rw_a2", "rw_g2", "rw_k_k", "rw_k_a",
                                "rw_r_k", "rw_ln_w", "rw_ln_b")]
    params += [cosf, sins, lp["gdn_conv_w"], lp["gdn_a_log"], lp["gdn_dt_bias"], lp["gdn_norm_w"], lb,
               lp["hg_norm_w"], consts["ones_bd"]]
    rows = lambda w: pl.BlockSpec((bt, w), lambda i: (i, 0))
    st4 = pl.BlockSpec((bt, NH, HD, HD), lambda i: (i, 0, 0, 0))
    cvs = pl.BlockSpec((bt, GDN_CONV - 1, CONV_CH), lambda i: (i, 0, 0))
    full = lambda t: pl.BlockSpec(t.shape, lambda i: (0,) * t.ndim)
    srw, ssh, sret, sgdn, scv, shg = states
    return pl.pallas_call(
        _step_body,
        grid=(n // bt,),
        in_specs=[rows(D_MODEL), rows(D_MODEL), rows(D_MODEL), rows(LANES), rows(D_MODEL),
                  st4, rows(D_MODEL), st4, st4, cvs, st4] + [full(t) for t in params],
        out_specs=[rows(BR)] * 4 + [st4, rows(D_MODEL), st4, st4, cvs, st4],
        out_shape=[jax.ShapeDtypeStruct((n, BR), BF)] * 4
        + [jax.ShapeDtypeStruct(srw.shape, F32), jax.ShapeDtypeStruct((n, D_MODEL), F32),
           jax.ShapeDtypeStruct(sret.shape, F32), jax.ShapeDtypeStruct(sgdn.shape, F32),
           jax.ShapeDtypeStruct(scv.shape, F32), jax.ShapeDtypeStruct(shg.shape, F32)],
        compiler_params=pltpu.CompilerParams(dimension_semantics=("arbitrary",),
                                             vmem_limit_bytes=VMEM_LIMIT),
        name="sample_step",
    )(prw, pret, pgdn, pab, phg, srw, ssh, sret, sgdn, scv, shg, *params)


def _constants():
    d = jnp.arange(BR)
    ones_bd = ((d[:, None] // HD) == (d[None, :] // HD)).astype(BF)
    consts = {"ones_bd": ones_bd, "eye": jnp.eye(HD, dtype=BF)}
    for nc in (1, 2, 4):
        i = jnp.arange(nc * CHUNK)
        tri = ((i[:, None] >= i[None, :]) & ((i[:, None] // CHUNK) == (i[None, :] // CHUNK))).astype(BF)
        consts["tri%d" % nc] = tri
        consts["triu%d" % nc] = tri.T
    consts["tri"] = consts["tri1"]
    return consts


def _layer_params(l, a):
    row = lambda t: t[l].reshape(1, -1).astype(F32)
    o_ret, o_gdn = D_MODEL, 2 * D_MODEL
    o_ab = o_gdn + 4 * BR
    o_hg = o_ab + 2 * NH
    o_gate = o_hg + 4 * BR
    w_in = a["w_in"][l]
    wab = jnp.pad(w_in[:, o_ab:o_hg], ((0, 0), (0, LANES - 2 * NH)))
    pad_row = lambda t: jnp.pad(t[l].reshape(1, -1).astype(F32), ((0, 0), (0, LANES - NH)))
    return {
        "norm_mix_w": row(a["norm_mix_w"]), "norm_ffn_w": row(a["norm_ffn_w"]),
        "w_mix": tuple(w.astype(BF) for w in (w_in[:, :o_ret], w_in[:, o_ret:o_gdn], w_in[:, o_gdn:o_ab], wab,
                                               w_in[:, o_hg:o_gate])),
        "w_gate": w_in[:, o_gate:].astype(BF), "w_branch": a["w_branch"][l].astype(BF),
        "w_out": a["w_out"][l].astype(BF), "w_gate_up": a["w_gate_up"][l].astype(BF),
        "w_down": a["w_down"][l].astype(BF),
        "rw_mu": row(a["rw_mu"]), "rw_w0": row(a["rw_w0"]), "rw_w2": a["rw_w2"][l].astype(BF),
        "rw_a0": row(a["rw_a0"]), "rw_a2": a["rw_a2"][l].astype(BF), "rw_g2": a["rw_g2"][l].astype(BF),
        "rw_k_k": row(a["rw_k_k"]), "rw_k_a": row(a["rw_k_a"]), "rw_r_k": row(a["rw_r_k"]),
        "rw_ln_w": row(a["rw_ln_w"]), "rw_ln_b": row(a["rw_ln_b"]),
        "gdn_conv_w": a["gdn_conv_w"][l].astype(F32), "gdn_a_log": pad_row(a["gdn_a_log"]),
        "gdn_dt_bias": pad_row(a["gdn_dt_bias"]),
        "gdn_norm_w": jnp.tile(a["gdn_norm_w"][l].reshape(1, HD).astype(F32), (1, NH)),
        "hg_norm_w": row(a["hg_norm_w"]),
    }


def kernel(x_prompt, x_sample, c_prompt, c_sample, state_rwkv, state_rwkv_shift, state_ret, state_gdn, state_gdn_conv, state_hgrn, ada_w, ada_b, norm_mix_w, w_in, rw_mu, rw_w0, rw_w2, rw_a0, rw_a2, rw_g2, rw_k_k, rw_k_a, rw_r_k, rw_ln_w, rw_ln_b, gdn_conv_w, gdn_a_log, gdn_dt_bias, gdn_norm_w, hg_lb_logits, hg_norm_w, w_branch, w_out, norm_ffn_w, w_gate_up, w_down, final_norm_w):
    a = dict(norm_mix_w=norm_mix_w, w_in=w_in, rw_mu=rw_mu, rw_w0=rw_w0, rw_w2=rw_w2, rw_a0=rw_a0, rw_a2=rw_a2,
             rw_g2=rw_g2, rw_k_k=rw_k_k, rw_k_a=rw_k_a, rw_r_k=rw_r_k, rw_ln_w=rw_ln_w, rw_ln_b=rw_ln_b,
             gdn_conv_w=gdn_conv_w, gdn_a_log=gdn_a_log, gdn_dt_bias=gdn_dt_bias, gdn_norm_w=gdn_norm_w,
             hg_norm_w=hg_norm_w, w_branch=w_branch, w_out=w_out, norm_ffn_w=norm_ffn_w, w_gate_up=w_gate_up,
             w_down=w_down)
    depth = ada_w.shape[0]
    Bp, Tp, _ = x_prompt.shape
    Bs = x_sample.shape[0]
    assert x_sample.shape[1] == 1 and Tp % CHUNK == 0
    tm = 256 if Tp % 256 == 0 else CHUNK
    consts = _constants()
    mod = _ada_mod(jnp.concatenate([c_prompt, c_sample], axis=0), ada_w.astype(F32), ada_b.astype(F32))
    lbs = _lower_bounds(hg_lb_logits)
    fw = final_norm_w.reshape(1, D_MODEL).astype(F32)

    xp = x_prompt
    xs = x_sample.reshape(1, Bs, D_MODEL)
    new_p = [[] for _ in range(6)]
    new_s = [[] for _ in range(6)]
    for l in range(depth):
        lp = _layer_params(l, a)
        lb = lbs[l:l + 1]
        mod_p = mod[l, :Bp].reshape(Bp, 1, -1)
        mod_s = mod[l, Bp:].reshape(1, Bs, -1)
        final = l == depth - 1

        h, prw, pret, pgdn, pab, phg = _in_proj(xp, mod_p, 1, tm, lp["norm_mix_w"], lp["w_mix"])
        y_rw, s_rw, s_sh = _rw_prompt(prw, lp, consts)
        y_ret, s_ret = _ret_prompt(pret, consts)
        y_gdn, s_gdn, s_cv = _gdn_prompt(pgdn, pab, lp["gdn_conv_w"], lp["gdn_a_log"], lp["gdn_dt_bias"],
                                         lp["gdn_norm_w"], consts)
        y_hg, s_hg = _hg_prompt(phg, lb, lp["hg_norm_w"], consts)
        xp = _merge(xp, h, (y_rw, y_ret, y_gdn, y_hg), mod_p, 1, tm, lp["w_gate"], lp["w_branch"], lp["w_out"])
        xp = _ffn(xp, mod_p, 1, tm, lp["norm_ffn_w"], lp["w_gate_up"], lp["w_down"], fw, final)
        for lst, s in zip(new_p, (s_rw, s_sh, s_ret, s_gdn, s_cv, s_hg)):
            lst.append(s)

        h, prw, pret, pgdn, pab, phg = _in_proj(xs, mod_s, Bs, Bs, lp["norm_mix_w"], lp["w_mix"])
        states = (state_rwkv[l], state_rwkv_shift[l].reshape(Bs, -1), state_ret[l], state_gdn[l],
                  state_gdn_conv[l], state_hgrn[l])
        outs = _step_sample(tuple(t[0] for t in (prw, pret, pgdn, pab, phg)), states, lp, lb, consts)
        ys = tuple(t[None] for t in outs[:4])
        xs = _merge(xs, h, ys, mod_s, Bs, Bs, lp["w_gate"], lp["w_branch"], lp["w_out"])
        xs = _ffn(xs, mod_s, Bs, Bs, lp["norm_ffn_w"], lp["w_gate_up"], lp["w_down"], fw, final)
        n_rw, n_sh, n_ret, n_gdn, n_cv, n_hg = outs[4:]
        for lst, s in zip(new_s, (n_rw, n_sh.reshape(Bs, 1, -1), n_ret, n_gdn, n_cv, n_hg)):
            lst.append(s)

    y_prompt = xp
    y_sample = xs.reshape(Bs, 1, D_MODEL)
    return (y_prompt, y_sample) + tuple(jnp.stack(t) for t in new_p) + tuple(jnp.stack(t) for t in new_s)
```

```python
import functools
import math

import jax
import jax.numpy as jnp
from jax import lax
from jax.experimental import pallas as pl
from jax.experimental.pallas import tpu as pltpu

F32 = jnp.float32
BF = jnp.bfloat16

D_MODEL = 1024
DEPTH = 4
PAST_LEN = 16384
BR = 256
HD = 64
NH = 4
RW_LORA_W = 64
RW_LORA_A = 64
RW_GN_EPS = 64e-5
ROPE_BASE = 10000.0
GDN_CONV = 4
CONV_CH = 3 * BR
D_FF = 2816
NORM_EPS = 1e-6
CHUNK = 64
SUB = 16
LANES = 128
VMEM_LIMIT = 56 * 1024 * 1024

NN = ((1,), (0,))
NT = ((1,), (1,))
TN = ((0,), (0,))


def _mm(a, b, dims=NN):
    return lax.dot_general(a.astype(BF), b.astype(BF), (dims, ((), ())), preferred_element_type=F32)


def _split2(a):
    hi = a.astype(BF)
    lo = (a - hi.astype(F32)).astype(BF)
    return hi, lo


def _split3(a):
    hi = a.astype(BF)
    r = a - hi.astype(F32)
    mid = r.astype(BF)
    lo = (r - mid.astype(F32)).astype(BF)
    return hi, mid, lo


def _mm_exact_rhs(a, b, dims=NN, parts=3):
    ps = _split3(a) if parts == 3 else _split2(a)
    out = _mm(ps[0], b, dims)
    for p in ps[1:]:
        out = out + _mm(p, b, dims)
    return out


def _mm_exact_lhs(a, b, dims=NN, parts=3):
    ps = _split3(b) if parts == 3 else _split2(b)
    out = _mm(a, ps[0], dims)
    for p in ps[1:]:
        out = out + _mm(a, p, dims)
    return out


def _sigmoid(x):
    return jax.nn.sigmoid(x)


def _silu(x):
    return x * jax.nn.sigmoid(x)


def _softplus(x):
    return jnp.maximum(x, 0.0) + jnp.log1p(jnp.exp(-jnp.abs(x)))


def _head_sum(x, ones_bd):
    return _mm(x, ones_bd)


def _tri_inverse(Ls, row, col):
    c = Ls[0].shape[0]
    eye = (row == col).astype(F32)
    bd = (row // SUB) == (col // SUB)
    Ps = [jnp.where(bd, L, 0.0) for L in Ls]
    Ds = [eye - P for P in Ps]
    for _ in range(int(math.log2(SUB)) - 1):
        Ps = [_mm(P, P) for P in Ps]
        Ds = [Dm + _mm(Dm, P) for Dm, P in zip(Ds, Ps)]
    size = SUB
    while size < c:
        off = ((row // (2 * size)) == (col // (2 * size))) & ((row // size) != (col // size))
        DCs = [_mm(Dm, jnp.where(off, L, 0.0)) for Dm, L in zip(Ds, Ls)]
        Ds = [Dm - _mm(DC, Dm) for Dm, DC in zip(Ds, DCs)]
        size *= 2
    return Ds


def _mod_body(c_ref, w_ref, b_ref, o_ref):
    o_ref[0] = _mm(_silu(c_ref[...]), w_ref[0]) + b_ref[0]


def _ada_mod(c_all, ada_w, ada_b):
    n = c_all.shape[0]
    ncol = ada_w.shape[2] // D_MODEL
    return pl.pallas_call(
        _mod_body,
        grid=(DEPTH, ncol),
        in_specs=[
            pl.BlockSpec((n, D_MODEL), lambda l, j: (0, 0)),
            pl.BlockSpec((1, D_MODEL, D_MODEL), lambda l, j: (l, 0, j)),
            pl.BlockSpec((1, 1, D_MODEL), lambda l, j: (l, 0, j)),
        ],
        out_specs=pl.BlockSpec((1, n, D_MODEL), lambda l, j: (l, 0, j)),
        out_shape=jax.ShapeDtypeStruct((DEPTH, n, ncol * D_MODEL), F32),
        compiler_params=pltpu.CompilerParams(dimension_semantics=("arbitrary", "arbitrary"),
                                             vmem_limit_bytes=VMEM_LIMIT),
        name="ada_mod",
    )(c_all, ada_w, ada_b.reshape(DEPTH, 1, -1))


def _lb_body(x_ref, o_ref):
    x = x_ref[...]
    m = jnp.max(x, axis=0, keepdims=True)
    e = jnp.exp(x - m)
    p = e / jnp.sum(e, axis=0, keepdims=True)
    acc = jnp.zeros_like(p[0:1])
    rows = [acc]
    for l in range(1, DEPTH):
        acc = acc + p[l:l + 1]
        rows.append(acc)
    o_ref[...] = jnp.concatenate(rows, axis=0)


def _lower_bounds(logits):
    return pl.pallas_call(
        _lb_body,
        out_shape=jax.ShapeDtypeStruct(logits.shape, F32),
        name="hgrn_lower_bounds",
    )(logits.astype(F32))


def _mod_spec(rm, tm, col):
    if rm == 1:
        return pl.BlockSpec((1, 1, D_MODEL), lambda b, i: (b, 0, col))
    return pl.BlockSpec((1, tm, D_MODEL), lambda b, i: (b, i, col))


def _full_spec(shape):
    nd = len(shape)
    return pl.BlockSpec(shape, lambda b, i: (0,) * nd)


def _row_spec(tm, width):
    return pl.BlockSpec((1, tm, width), lambda b, i: (b, i, 0))


def _modulated_norm(x, nw, sc, sh):
    ms = jnp.mean(x * x, axis=-1, keepdims=True)
    return (x * lax.rsqrt(ms + NORM_EPS) * nw) * (1.0 + sc) + sh


def _in_body(x_ref, sc_ref, sh_ref, nw_ref, wrw_ref, wret_ref, wgdn_ref, wab_ref, whg_ref,
             h_ref, prw_ref, pret_ref, pgdn_ref, pab_ref, phg_ref):
    h = _modulated_norm(x_ref[0], nw_ref[...], sc_ref[0], sh_ref[0]).astype(BF)
    h_ref[0] = h
    prw_ref[0] = _mm(h, wrw_ref[...])
    pret_ref[0] = _mm(h, wret_ref[...])
    pgdn_ref[0] = _mm(h, wgdn_ref[...])
    pab_ref[0] = _mm(h, wab_ref[...])
    phg_ref[0] = _mm(h, whg_ref[...])


def _in_proj(x, mod, rm, tm, nw, ws):
    G, R, _ = x.shape
    wrw, wret, wgdn, wab, whg = ws
    outs = [jax.ShapeDtypeStruct((G, R, D_MODEL), BF)] + [
        jax.ShapeDtypeStruct((G, R, w.shape[1]), F32) for w in ws]
    return pl.pallas_call(
        _in_body,
        grid=(G, R // tm),
        in_specs=[_row_spec(tm, D_MODEL), _mod_spec(rm, tm, 1), _mod_spec(rm, tm, 0),
                  _full_spec((1, D_MODEL))] + [_full_spec(w.shape) for w in ws],
        out_specs=[_row_spec(tm, D_MODEL)] + [_row_spec(tm, w.shape[1]) for w in ws],
        out_shape=outs,
        compiler_params=pltpu.CompilerParams(dimension_semantics=("arbitrary", "arbitrary"),
                                             vmem_limit_bytes=VMEM_LIMIT),
        name="in_proj",
    )(x, mod, mod, nw, wrw, wret, wgdn, wab, whg)


def _merge_body(x_ref, h_ref, yrw_ref, yret_ref, ygdn_ref, yhg_ref, g1_ref, wg_ref, wb_ref, wo_ref,
                o_ref):
    h = h_ref[0]
    merged = None
    for n, y_ref in enumerate((yrw_ref, yret_ref, ygdn_ref, yhg_ref)):
        gate = _sigmoid(_mm(h, wg_ref[:, n * D_MODEL:(n + 1) * D_MODEL]))
        term = gate * _mm(y_ref[0], wb_ref[n])
        merged = term if merged is None else merged + term
    o_ref[0] = x_ref[0] + g1_ref[0] * _mm(merged, wo_ref[...])


def _merge(x, h, ys, mod, rm, tm, wg, wb, wo):
    G, R, _ = x.shape
    return pl.pallas_call(
        _merge_body,
        grid=(G, R // tm),
        in_specs=[_row_spec(tm, D_MODEL), _row_spec(tm, D_MODEL)] + [_row_spec(tm, BR)] * 4
        + [_mod_spec(rm, tm, 2), _full_spec(wg.shape), _full_spec(wb.shape), _full_spec(wo.shape)],
        out_specs=_row_spec(tm, D_MODEL),
        out_shape=jax.ShapeDtypeStruct((G, R, D_MODEL), F32),
        compiler_params=pltpu.CompilerParams(dimension_semantics=("arbitrary", "arbitrary"),
                                             vmem_limit_bytes=VMEM_LIMIT),
        name="merge_out",
    )(x, h, *ys, mod, wg, wb, wo)


def _ffn_body(final, x_ref, sc_ref, sh_ref, g2_ref, nw_ref, wgu_ref, wd_ref, fw_ref, o_ref):
    x = x_ref[0]
    h = _modulated_norm(x, nw_ref[...], sc_ref[0], sh_ref[0]).astype(BF)
    gate = _mm(h, wgu_ref[:, :D_FF])
    up = _mm(h, wgu_ref[:, D_FF:])
    x = x + g2_ref[0] * _mm(_silu(gate) * up, wd_ref[...])
    if final:
        ms = jnp.mean(x * x, axis=-1, keepdims=True)
        x = x * lax.rsqrt(ms + NORM_EPS) * fw_ref[...]
    o_ref[0] = x


def _ffn(x, mod, rm, tm, nw, wgu, wd, fw, final):
    G, R, _ = x.shape
    return pl.pallas_call(
        functools.partial(_ffn_body, final),
        grid=(G, R // tm),
        in_specs=[_row_spec(tm, D_MODEL), _mod_spec(rm, tm, 4), _mod_spec(rm, tm, 3), _mod_spec(rm, tm, 5),
                  _full_spec((1, D_MODEL)), _full_spec(wgu.shape), _full_spec(wd.shape),
                  _full_spec((1, D_MODEL))],
        out_specs=_row_spec(tm, D_MODEL),
        out_shape=jax.ShapeDtypeStruct((G, R, D_MODEL), F32),
        compiler_params=pltpu.CompilerParams(dimension_semantics=("arbitrary", "arbitrary"),
                                             vmem_limit_bytes=VMEM_LIMIT),
        name="ffn",
    )(x, mod, mod, mod, nw, wgu, wd, fw)


def _chunk_spec(c, width):
    return pl.BlockSpec((1, c, width), lambda b, i: (b, i, 0))


def _state_spec():
    return pl.BlockSpec((1, NH, HD, HD), lambda b, i: (b, 0, 0, 0))


def _iota2(shape):
    return (lax.broadcasted_iota(jnp.int32, shape, 0), lax.broadcasted_iota(jnp.int32, shape, 1))


def _rope(t, cosf, sins):
    lane = lax.broadcasted_iota(jnp.int32, t.shape, 1)
    width = t.shape[1]
    rot = jnp.where((lane % HD) < HD // 2, pltpu.roll(t, width - HD // 2, 1), pltpu.roll(t, HD // 2, 1))
    return t * cosf + rot * sins


def _ret_body(nc, p_ref, cos_ref, sin_ref, qin_ref, kout_ref, dmat_ref, cdec_ref, ones_ref, y_ref, s_ref):
    c = CHUNK

    @pl.when(pl.program_id(1) == 0)
    def _():
        s_ref[...] = jnp.zeros_like(s_ref)

    P = p_ref[0]
    cosf, sins = cos_ref[...], sin_ref[...]
    q = _rope(P[:, :BR], cosf, sins)
    k = _rope(P[:, BR:2 * BR], cosf, sins) * HD ** -0.5
    v = P[:, 2 * BR:3 * BR]
    g = P[:, 3 * BR:]
    qs = q * qin_ref[...]
    ko = k * kout_ref[...]
    items = [(ci, h) for ci in range(nc) for h in range(NH)]

    def sl(t, ci, h):
        return t[ci * c:(ci + 1) * c, h * HD:(h + 1) * HD]

    ss = [_mm(sl(q, ci, h), sl(k, ci, h), NT) * dmat_ref[h] for ci, h in items]
    oi = [_mm(s, sl(v, ci, h)) for s, (ci, h) in zip(ss, items)]
    kv = [_mm(sl(ko, ci, h), sl(v, ci, h), TN) for ci, h in items]
    outs = [[None] * NH for _ in range(nc)]
    for idx, (ci, h) in enumerate(items):
        S = s_ref[0, h]
        outs[ci][h] = oi[idx] + _mm(sl(qs, ci, h), S)
        s_ref[0, h] = S * cdec_ref[h] + kv[idx]
    o = jnp.concatenate([jnp.concatenate(r, axis=1) for r in outs], axis=0)
    ms = _head_sum(o * o, ones_ref[...]) * (1.0 / HD)
    y_ref[0] = (o * lax.rsqrt(ms + NORM_EPS) * _silu(g)).astype(BF)


def _ret_tables(c):
    lg = jnp.log1p(-jnp.exp2(-5.0 - jnp.arange(NH, dtype=F32)))[:, None]
    idx = jnp.arange(c, dtype=F32)
    diff = idx[:, None] - idx[None, :]
    dmat = jnp.where(diff >= 0, jnp.exp(lg[:, :, None] * jnp.maximum(diff, 0.0)), 0.0)
    q_in = jnp.exp(lg * (idx + 1.0))
    k_out = jnp.exp(lg * (c - 1.0 - idx))
    c_dec = jnp.exp(lg * c)
    expand = lambda t: jnp.repeat(t.T, HD, axis=1)
    return dmat, expand(q_in), expand(k_out), jnp.broadcast_to(c_dec[:, :, None], (NH, HD, HD))


def _rope_tables(pos):
    half = HD // 2
    freqs = ROPE_BASE ** (-jnp.arange(half, dtype=F32) / half)
    ang = pos.astype(F32)[:, None] * freqs
    cos, sin = jnp.cos(ang), jnp.sin(ang)
    cosf = jnp.tile(jnp.concatenate([cos, cos], axis=1), (1, NH))
    sins = jnp.tile(jnp.concatenate([-sin, sin], axis=1), (1, NH))
    return cosf, sins


def _ret_prompt(p, consts):
    B, T, _ = p.shape
    c = CHUNK
    nc = _chunks_per_step(T)
    tb = nc * c
    cosf, sins = _rope_tables(jnp.arange(T))
    dmat, q_in, k_out, c_dec = _ret_tables(c)
    q_in, k_out = jnp.tile(q_in, (nc, 1)), jnp.tile(k_out, (nc, 1))
    return pl.pallas_call(
        functools.partial(_ret_body, nc),
        grid=(B, T // tb),
        in_specs=[_chunk_spec(tb, D_MODEL),
                  pl.BlockSpec((tb, BR), lambda b, i: (i, 0)), pl.BlockSpec((tb, BR), lambda b, i: (i, 0)),
                  _full_spec((tb, BR)), _full_spec((tb, BR)), _full_spec((NH, c, c)), _full_spec((NH, HD, HD)),
                  _full_spec((BR, BR))],
        out_specs=[_chunk_spec(tb, BR), _state_spec()],
        out_shape=[jax.ShapeDtypeStruct((B, T, BR), BF), jax.ShapeDtypeStruct((B, NH, HD, HD), F32)],
        compiler_params=pltpu.CompilerParams(dimension_semantics=("arbitrary", "arbitrary"),
                                             vmem_limit_bytes=VMEM_LIMIT),
        name="ret_prompt",
    )(p, cosf, sins, q_in, k_out, dmat, c_dec, consts["ones_bd"])


def _chunks_per_step(T):
    for nc in (4, 2):
        if T % (nc * CHUNK) == 0:
            return nc
    return 1


def _gdn_body(nc, pg_ref, pab_ref, cw_ref, alog_ref, dtb_ref, nw_ref, tri_ref, triu_ref, ones_ref,
              y_ref, s_ref, cv_ref, xext):
    c = CHUNK
    tb = nc * c

    @pl.when(pl.program_id(1) == 0)
    def _():
        s_ref[...] = jnp.zeros_like(s_ref)
        xext[pl.ds(0, 8), :] = jnp.zeros((8, CONV_CH), F32)

    P = pg_ref[0]
    X = P[:, :CONV_CH]
    xext[pl.ds(8, tb), :] = X
    cw = cw_ref[...]
    conv = xext[pl.ds(8 - 3, tb), :] * cw[0:1]
    conv = conv + xext[pl.ds(8 - 2, tb), :] * cw[1:2]
    conv = conv + xext[pl.ds(8 - 1, tb), :] * cw[2:3]
    conv = conv + X * cw[3:4]
    xext[pl.ds(0, 8), :] = X[tb - 8:tb]
    cv_ref[0] = X[tb - (GDN_CONV - 1):tb]
    conv = _silu(conv)
    ones_bd = ones_ref[...]
    qn, kn, v = conv[:, :BR], conv[:, BR:2 * BR], conv[:, 2 * BR:]
    q = qn * lax.rsqrt(_head_sum(qn * qn, ones_bd) + 1e-6) * HD ** -0.5
    k = kn * lax.rsqrt(_head_sum(kn * kn, ones_bd) + 1e-6)
    gate = P[:, CONV_CH:]

    ab = pab_ref[0]
    glog = -jnp.exp(alog_ref[...]) * _softplus(ab + dtb_ref[...])
    beta = _sigmoid(ab)
    Gc = _mm_exact_lhs(tri_ref[...], glog, NN)
    Gr = _mm_exact_rhs(glog, triu_ref[...], TN)
    eG = jnp.exp(Gc)
    row, col = _iota2((c, c))
    incl = row >= col
    strict = row > col
    items = [(ci, h) for ci in range(nc) for h in range(NH)]

    def sl(t, ci, h):
        return t[ci * c:(ci + 1) * c, h * HD:(h + 1) * HD]

    KQ = [_mm(jnp.concatenate([sl(k, ci, h), sl(q, ci, h)], axis=0), sl(k, ci, h), NT) for ci, h in items]
    Ls, qks, rhss, kouts, cdecs = [], [], [], [], []
    for (ci, h), kq in zip(items, KQ):
        rows = slice(ci * c, (ci + 1) * c)
        gc = Gc[rows, h:h + 1]
        gr = Gr[h:h + 1, rows]
        bcol = beta[rows, NH + h:NH + h + 1]
        gam = jnp.where(incl, jnp.exp(jnp.where(incl, gc - gr, 0.0)), 0.0)
        Ls.append(jnp.where(strict, bcol * gam * kq[:c], 0.0))
        qks.append(kq[c:] * gam)
        rhss.append(jnp.concatenate([sl(v, ci, h) * bcol, sl(k, ci, h) * (bcol * eG[rows, h:h + 1])], axis=1))
        gend = gc[c - 1:c, :]
        kouts.append(sl(k, ci, h) * jnp.exp(gend - gc))
        cdecs.append(jnp.exp(gend))
    Ts = _tri_inverse(Ls, row, col)
    sols = [_mm(T, rhs) for T, rhs in zip(Ts, rhss)]
    QS = [_mm(qk, sol) for qk, sol in zip(qks, sols)]
    KS = [_mm(kout, sol, TN) for kout, sol in zip(kouts, sols)]
    outs = [[None] * NH for _ in range(nc)]
    for idx, (ci, h) in enumerate(items):
        rows = slice(ci * c, (ci + 1) * c)
        QW = sl(q, ci, h) * eG[rows, h:h + 1] - QS[idx][:, HD:]
        S = s_ref[0, h]
        Z = _mm(jnp.concatenate([KS[idx][:, HD:], QW], axis=0), S)
        s_ref[0, h] = S * cdecs[idx] - Z[:HD] + KS[idx][:, :HD]
        outs[ci][h] = Z[HD:] + QS[idx][:, :HD]
    o = jnp.concatenate([jnp.concatenate(r, axis=1) for r in outs], axis=0)
    ms = _head_sum(o * o, ones_bd) * (1.0 / HD)
    y_ref[0] = (o * lax.rsqrt(ms + NORM_EPS) * nw_ref[...] * _silu(gate)).astype(BF)


def _gdn_prompt(pg, pab, cw, alog, dtb, nw, consts):
    B, T, _ = pg.shape
    nc = _chunks_per_step(T)
    tb = nc * CHUNK
    return pl.pallas_call(
        functools.partial(_gdn_body, nc),
        grid=(B, T // tb),
        in_specs=[_chunk_spec(tb, D_MODEL), _chunk_spec(tb, LANES), _full_spec((GDN_CONV, CONV_CH)),
                  _full_spec((1, LANES)), _full_spec((1, LANES)), _full_spec((1, BR)),
                  _full_spec((tb, tb)), _full_spec((tb, tb)), _full_spec((BR, BR))],
        out_specs=[_chunk_spec(tb, BR), _state_spec(),
                   pl.BlockSpec((1, GDN_CONV - 1, CONV_CH), lambda b, i: (b, 0, 0))],
        out_shape=[jax.ShapeDtypeStruct((B, T, BR), BF), jax.ShapeDtypeStruct((B, NH, HD, HD), F32),
                   jax.ShapeDtypeStruct((B, GDN_CONV - 1, CONV_CH), F32)],
        scratch_shapes=[pltpu.VMEM((tb + 8, CONV_CH), F32)],
        compiler_params=pltpu.CompilerParams(dimension_semantics=("arbitrary", "arbitrary"),
                                             vmem_limit_bytes=VMEM_LIMIT),
        name="gdn_prompt",
    )(pg, pab, cw, alog, dtb, nw, consts["tri%d" % nc], consts["triu%d" % nc], consts["ones_bd"])


def _hgrn_gates(xq, xf, lb):
    q = _silu(xq)
    ls = jnp.minimum(xf, 0.0) - jnp.log1p(jnp.exp(-jnp.abs(xf)))
    pos = lb > 0.0
    a = jnp.log(jnp.where(pos, lb, 1.0))
    b = jnp.log1p(-lb) + ls
    lae = jnp.maximum(a, b) + jnp.log1p(jnp.exp(-jnp.abs(a - b)))
    log_f = jnp.where(pos, lae, ls)
    k = (1.0 - lb) * _sigmoid(-xf)
    return q, log_f, k


def _hg_body(nc, p_ref, lb_ref, nw_ref, tri_ref, ones_ref, eye_ref, y_ref, s_ref, st, tbuf, sbuf):
    c = CHUNK
    tb = nc * c
    nb = tb // SUB
    half = SUB // 2
    gi = pl.program_id(1)

    @pl.when(gi == 0)
    def _():
        st[...] = jnp.zeros_like(st)

    P = p_ref[0]
    q, log_f, k = _hgrn_gates(P[:, :BR], P[:, BR:2 * BR], lb_ref[...])
    v = P[:, 2 * BR:3 * BR]
    og = P[:, 3 * BR:]
    ones_bd = ones_ref[...]
    G = _mm_exact_lhs(tri_ref[...], log_f, NN)

    G3, q3, k3, v3 = (t.reshape(nb, SUB, BR) for t in (G, q, k, v))
    o_lo = jnp.zeros((nb, half, BR), F32)
    o_hi = jnp.zeros((nb, half, BR), F32)
    offs, total = [], 0
    for jj in range(SUB):
        offs.append(total)
        total += nb * (SUB if jj < half else half)
    for jj in range(SUB):
        i0 = 0 if jj < half else half
        rowi = lax.broadcasted_iota(jnp.int32, (nb, SUB - i0, BR), 1) + i0
        e = jnp.exp(jnp.minimum(G3[:, i0:, :] - G3[:, jj:jj + 1, :], 0.0))
        t = jnp.where(rowi >= jj, q3[:, i0:, :] * e * k3[:, jj:jj + 1, :], 0.0)
        tbuf[pl.ds(offs[jj], nb * (SUB - i0)), :] = t.reshape(nb * (SUB - i0), BR).astype(BF)
    sbuf[...] = jnp.dot(tbuf[...], ones_bd, preferred_element_type=F32)
    for jj in range(SUB):
        i0 = 0 if jj < half else half
        s = sbuf[pl.ds(offs[jj], nb * (SUB - i0)), :].reshape(nb, SUB - i0, BR)
        contrib = s * v3[:, jj:jj + 1, :]
        if i0 == 0:
            o_lo = o_lo + contrib[:, :half]
            o_hi = o_hi + contrib[:, half:]
        else:
            o_hi = o_hi + contrib
    o = jnp.concatenate([o_lo, o_hi], axis=1).reshape(tb, BR)

    pairs = [(ci, bi) for ci in range(nc) for bi in range(1, c // SUB)]
    qts, kts = [], []
    for ci, bi in pairs:
        r0 = ci * c
        ref = G[r0 + bi * SUB - 1:r0 + bi * SUB, :]
        rows_i = slice(r0 + bi * SUB, r0 + (bi + 1) * SUB)
        rows_j = slice(r0, r0 + bi * SUB)
        qts.append(q[rows_i] * jnp.exp(jnp.minimum(G[rows_i] - ref, 0.0)))
        kts.append(k[rows_j] * jnp.exp(jnp.minimum(ref - G[rows_j], 0.0)))
    wts = [[_mm(qt[:, h * HD:(h + 1) * HD], kt[:, h * HD:(h + 1) * HD], NT) for h in range(NH)]
           for qt, kt in zip(qts, kts)]
    offs = [[_mm(w[h], v[ci * c:ci * c + bi * SUB, h * HD:(h + 1) * HD]) for h in range(NH)]
            for (ci, bi), w in zip(pairs, wts)]
    off_rows = []
    idx = 0
    for ci in range(nc):
        off_rows.append(jnp.zeros((SUB, BR), F32))
        for bi in range(1, c // SUB):
            off_rows.append(jnp.concatenate(offs[idx], axis=1))
            idx += 1
    o = o + jnp.concatenate(off_rows, axis=0)

    qg = q * jnp.exp(G)
    GE = jnp.concatenate([jnp.broadcast_to(G[(ci + 1) * c - 1:(ci + 1) * c], (c, BR)) for ci in range(nc)], axis=0)
    kd = k * jnp.exp(GE - G)
    items = [(ci, h) for ci in range(nc) for h in range(NH)]

    def sl(t, ci, h):
        return t[ci * c:(ci + 1) * c, h * HD:(h + 1) * HD]

    vk = [_mm(sl(v, ci, h), sl(kd, ci, h), TN) for ci, h in items]
    outs = [[None] * NH for _ in range(nc)]
    for idx, (ci, h) in enumerate(items):
        ST = st[h]
        outs[ci][h] = _mm(sl(qg, ci, h), ST, NT)
        st[h] = ST * jnp.exp(G[(ci + 1) * c - 1:(ci + 1) * c, h * HD:(h + 1) * HD]) + vk[idx]
    o = o + jnp.concatenate([jnp.concatenate(r, axis=1) for r in outs], axis=0)
    ms = _head_sum(o * o, ones_bd) * (1.0 / HD)
    y_ref[0] = (o * lax.rsqrt(ms + NORM_EPS) * nw_ref[...] * _sigmoid(og)).astype(BF)

    @pl.when(gi == pl.num_programs(1) - 1)
    def _():
        for h in range(NH):
            s_ref[0, h] = _mm_exact_lhs(eye_ref[...], st[h], NT)


def _hg_prompt(p, lb, nw, consts):
    B, T, _ = p.shape
    nc = _chunks_per_step(T)
    tb = nc * CHUNK
    return pl.pallas_call(
        functools.partial(_hg_body, nc),
        grid=(B, T // tb),
        in_specs=[_chunk_spec(tb, D_MODEL), _full_spec((1, BR)), _full_spec((1, BR)), _full_spec((tb, tb)),
                  _full_spec((BR, BR)), _full_spec((HD, HD))],
        out_specs=[_chunk_spec(tb, BR), _state_spec()],
        out_shape=[jax.ShapeDtypeStruct((B, T, BR), BF), jax.ShapeDtypeStruct((B, NH, HD, HD), F32)],
        scratch_shapes=[pltpu.VMEM((NH, HD, HD), F32),
                        pltpu.VMEM((tb // SUB * (SUB + SUB // 2) * (SUB // 2), BR), BF),
                        pltpu.VMEM((tb // SUB * (SUB + SUB // 2) * (SUB // 2), BR), F32)],
        compiler_params=pltpu.CompilerParams(dimension_semantics=("arbitrary", "arbitrary"),
                                             vmem_limit_bytes=VMEM_LIMIT),
        name="hgrn_prompt",
    )(p, lb, nw, consts["tri%d" % nc], consts["ones_bd"], consts["eye"])


def _rw_features(P, prev, mu, w0, w2, a0, a2, g2, kkw, kaw, ones_bd):
    Pm = P + mu * (prev - P)
    r, kx, vx = Pm[:, :BR], Pm[:, BR:2 * BR], Pm[:, 2 * BR:3 * BR]
    o4 = 3 * BR + RW_LORA_W
    o5 = o4 + RW_LORA_A
    wl, al, gl = Pm[:, 3 * BR:o4], Pm[:, o4:o5], Pm[:, o5:]
    zw = w0 + _mm(jnp.tanh(wl), w2)
    lw = -jnp.exp(-_softplus(-zw) - 0.5)
    a = _sigmoid(a0 + _mm(al, a2))
    g = _mm(_sigmoid(gl), g2)
    kks = kx * kkw
    kk = kks * lax.rsqrt(_head_sum(kks * kks, ones_bd) + 1e-6)
    k = kx * (1.0 + (a - 1.0) * kaw)
    return r, lw, k, vx, -kk, kk * a, g


def _rw_output(o, r, k, vx, g, rkw, lnw, lnb, ones_bd):
    mean = _head_sum(o, ones_bd) * (1.0 / HD)
    cen = o - mean
    var = _head_sum(cen * cen, ones_bd) * (1.0 / HD)
    on = cen * lax.rsqrt(var + RW_GN_EPS) * lnw + lnb
    bonus = _head_sum(r * k * rkw, ones_bd) * vx
    return (on + bonus) * g


def _rw_body(nc, p_ref, mu_ref, w0_ref, w2_ref, a0_ref, a2_ref, g2_ref, kkw_ref, kaw_ref, rkw_ref,
             lnw_ref, lnb_ref, tri_ref, ones_ref, eye_ref, y_ref, s_ref, sh_ref, st, prev):
    c = CHUNK
    tb = nc * c
    gi = pl.program_id(1)

    @pl.when(gi == 0)
    def _():
        st[...] = jnp.zeros_like(st)
        prev[...] = jnp.zeros_like(prev)

    P = p_ref[0]
    rowp = lax.broadcasted_iota(jnp.int32, P.shape, 0)
    shifted = jnp.where(rowp == 0, prev[0:1, :], pltpu.roll(P, 1, 0))
    prev[0:1, :] = P[tb - 1:tb, :]
    sh_ref[0] = P[tb - 1:tb, :]
    ones_bd = ones_ref[...]
    r, lw, k, vx, av, bv, g = _rw_features(P, shifted, mu_ref[...], w0_ref[...], w2_ref[...], a0_ref[...],
                                           a2_ref[...], g2_ref[...], kkw_ref[...], kaw_ref[...], ones_bd)
    LG = _mm_exact_lhs(tri_ref[...], lw, NN)
    LGE = jnp.concatenate([jnp.broadcast_to(LG[(ci + 1) * c - 1:(ci + 1) * c], (c, BR)) for ci in range(nc)],
                          axis=0)
    rt = r * jnp.exp(LG)
    at = av * jnp.exp(LG - lw)
    einv = jnp.exp(-LG)
    kh_ = k * einv
    bh_ = bv * einv
    eo = jnp.exp(LGE - LG)
    kout = k * eo
    bout = bv * eo
    row, col = _iota2((c, c))
    incl = row >= col
    strict = row > col
    items = [(ci, h) for ci in range(nc) for h in range(NH)]

    def sl(t, ci, h):
        return t[ci * c:(ci + 1) * c, h * HD:(h + 1) * HD]

    Ms = [_mm(jnp.concatenate([sl(at, ci, h), sl(rt, ci, h)], axis=0),
              jnp.concatenate([sl(bh_, ci, h), sl(kh_, ci, h)], axis=0), NT) for ci, h in items]
    Aabs = [jnp.where(strict, M[:c, :c], 0.0) for M in Ms]
    Arbs = [jnp.where(incl, M[c:, :c], 0.0) for M in Ms]
    AKs = [jnp.concatenate([jnp.where(strict, M[:c, c:], 0.0), jnp.where(incl, M[c:, c:], 0.0)], axis=0)
           for M in Ms]
    AVs = [_mm(AK, sl(vx, ci, h)) for AK, (ci, h) in zip(AKs, items)]
    Ts = _tri_inverse([-A for A in Aabs], row, col)
    WTs = [_mm(T, jnp.concatenate([AV[:c], sl(at, ci, h)], axis=1))
           for T, AV, (ci, h) in zip(Ts, AVs, items)]
    AWs = [_mm(Arb, WT) for Arb, WT in zip(Arbs, WTs)]
    CBs = [_mm(WT, sl(bout, ci, h), TN) for WT, (ci, h) in zip(WTs, items)]
    VKs = [_mm(sl(vx, ci, h), sl(kout, ci, h), TN) for ci, h in items]
    outs = [[None] * NH for _ in range(nc)]
    for idx, (ci, h) in enumerate(items):
        dec = jnp.exp(LG[(ci + 1) * c - 1:(ci + 1) * c, h * HD:(h + 1) * HD])
        RQ = sl(rt, ci, h) + AWs[idx][:, HD:]
        ST = st[h]
        outs[ci][h] = _mm(RQ, ST, NT) + AWs[idx][:, :HD] + AVs[idx][c:]
        st[h] = ST * dec + _mm(ST, CBs[idx][HD:]) + CBs[idx][:HD] + VKs[idx]
    o = jnp.concatenate([jnp.concatenate(rw, axis=1) for rw in outs], axis=0)
    y_ref[0] = _rw_output(o, r, k, vx, g, rkw_ref[...], lnw_ref[...], lnb_ref[...], ones_bd).astype(BF)

    @pl.when(gi == pl.num_programs(1) - 1)
    def _():
        for h in range(NH):
            s_ref[0, h] = _mm_exact_lhs(eye_ref[...], st[h], NT)


def _rw_prompt(p, lp, consts):
    B, T, _ = p.shape
    nc = _chunks_per_step(T)
    tb = nc * CHUNK
    params = [lp[n] for n in ("rw_mu", "rw_w0", "rw_w2", "rw_a0", "rw_a2", "rw_g2", "rw_k_k", "rw_k_a",
                              "rw_r_k", "rw_ln_w", "rw_ln_b")]
    return pl.pallas_call(
        functools.partial(_rw_body, nc),
        grid=(B, T // tb),
        in_specs=[_chunk_spec(tb, D_MODEL)] + [_full_spec(t.shape) for t in params]
        + [_full_spec((tb, tb)), _full_spec((BR, BR)), _full_spec((HD, HD))],
        out_specs=[_chunk_spec(tb, BR), _state_spec(),
                   pl.BlockSpec((1, 1, D_MODEL), lambda b, i: (b, 0, 0))],
        out_shape=[jax.ShapeDtypeStruct((B, T, BR), BF), jax.ShapeDtypeStruct((B, NH, HD, HD), F32),
                   jax.ShapeDtypeStruct((B, 1, D_MODEL), F32)],
        scratch_shapes=[pltpu.VMEM((NH, HD, HD), F32), pltpu.VMEM((8, D_MODEL), F32)],
        compiler_params=pltpu.CompilerParams(dimension_semantics=("arbitrary", "arbitrary"),
                                             vmem_limit_bytes=VMEM_LIMIT),
        name="rwkv_prompt",
    )(p, *params, consts["tri%d" % nc], consts["ones_bd"], consts["eye"])


def _col(x):
    return x[:, :, None]


def _rowv(x):
    return x[:, None, :]


def _step_body(prw_ref, pret_ref, pgdn_ref, pab_ref, phg_ref,
               srw_ref, ssh_ref, sret_ref, sgdn_ref, scv_ref, shg_ref,
               mu_ref, w0_ref, w2_ref, a0_ref, a2_ref, g2_ref, kkw_ref, kaw_ref, rkw_ref, lnw_ref, lnb_ref,
               cos_ref, sin_ref, cw_ref, alog_ref, dtb_ref, gnw_ref, lb_ref, hnw_ref, ones_ref,
               yrw_ref, yret_ref, ygdn_ref, yhg_ref,
               nrw_ref, nsh_ref, nret_ref, ngdn_ref, ncv_ref, nhg_ref):
    ones_bd = ones_ref[...]

    P = prw_ref[...]
    nsh_ref[...] = P
    r, lw, k, vx, av, bv, g = _rw_features(P, ssh_ref[...], mu_ref[...], w0_ref[...], w2_ref[...], a0_ref[...],
                                           a2_ref[...], g2_ref[...], kkw_ref[...], kaw_ref[...], ones_bd)
    w = jnp.exp(lw)
    outs = []
    for h in range(NH):
        sl = slice(h * HD, (h + 1) * HD)
        S = srw_ref[:, h]
        sa = jnp.sum(S * _col(av[:, sl]), axis=1, keepdims=True)
        S = _col(w[:, sl]) * S + _col(bv[:, sl]) * sa + _col(k[:, sl]) * _rowv(vx[:, sl])
        nrw_ref[:, h] = S
        outs.append(jnp.sum(S * _col(r[:, sl]), axis=1))
    o = jnp.concatenate(outs, axis=1)
    yrw_ref[...] = _rw_output(o, r, k, vx, g, rkw_ref[...], lnw_ref[...], lnb_ref[...], ones_bd).astype(BF)

    P = pret_ref[...]
    cosf, sins = cos_ref[...], sin_ref[...]
    q = _rope(P[:, :BR], cosf, sins)
    k = _rope(P[:, BR:2 * BR], cosf, sins) * HD ** -0.5
    v = P[:, 2 * BR:3 * BR]
    outs = []
    for h in range(NH):
        sl = slice(h * HD, (h + 1) * HD)
        S = sret_ref[:, h] * (1.0 - 2.0 ** (-5 - h)) + _col(k[:, sl]) * _rowv(v[:, sl])
        nret_ref[:, h] = S
        outs.append(jnp.sum(S * _col(q[:, sl]), axis=1))
    o = jnp.concatenate(outs, axis=1)
    ms = _head_sum(o * o, ones_bd) * (1.0 / HD)
    yret_ref[...] = (o * lax.rsqrt(ms + NORM_EPS) * _silu(P[:, 3 * BR:])).astype(BF)

    P = pgdn_ref[...]
    X = P[:, :CONV_CH]
    cw = cw_ref[...]
    conv = scv_ref[:, 0, :] * cw[0:1] + scv_ref[:, 1, :] * cw[1:2] + scv_ref[:, 2, :] * cw[2:3] + X * cw[3:4]
    ncv_ref[:, 0, :] = scv_ref[:, 1, :]
    ncv_ref[:, 1, :] = scv_ref[:, 2, :]
    ncv_ref[:, 2, :] = X
    conv = _silu(conv)
    qn, kn, v = conv[:, :BR], conv[:, BR:2 * BR], conv[:, 2 * BR:]
    q = qn * lax.rsqrt(_head_sum(qn * qn, ones_bd) + 1e-6) * HD ** -0.5
    k = kn * lax.rsqrt(_head_sum(kn * kn, ones_bd) + 1e-6)
    ab = pab_ref[...]
    eg = jnp.exp(-jnp.exp(alog_ref[...]) * _softplus(ab + dtb_ref[...]))
    beta = _sigmoid(ab)
    outs = []
    for h in range(NH):
        sl = slice(h * HD, (h + 1) * HD)
        egh = eg[:, h:h + 1][:, :, None]
        bh = beta[:, NH + h:NH + h + 1][:, :, None]
        S = sgdn_ref[:, h]
        kS = jnp.sum(S * _col(k[:, sl]), axis=1, keepdims=True)
        u = bh * (_rowv(v[:, sl]) - egh * kS)
        S = egh * S + _col(k[:, sl]) * u
        ngdn_ref[:, h] = S
        outs.append(jnp.sum(S * _col(q[:, sl]), axis=1))
    o = jnp.concatenate(outs, axis=1)
    ms = _head_sum(o * o, ones_bd) * (1.0 / HD)
    ygdn_ref[...] = (o * lax.rsqrt(ms + NORM_EPS) * gnw_ref[...] * _silu(P[:, CONV_CH:])).astype(BF)

    P = phg_ref[...]
    q, log_f, k = _hgrn_gates(P[:, :BR], P[:, BR:2 * BR], lb_ref[...])
    f = jnp.exp(log_f)
    v = P[:, 2 * BR:3 * BR]
    outs = []
    for h in range(NH):
        sl = slice(h * HD, (h + 1) * HD)
        S = shg_ref[:, h] * _col(f[:, sl]) + _col(k[:, sl]) * _rowv(v[:, sl])
        nhg_ref[:, h] = S
        outs.append(jnp.sum(S * _col(q[:, sl]), axis=1))
    o = jnp.concatenate(outs, axis=1)
    ms = _head_sum(o * o, ones_bd) * (1.0 / HD)
    yhg_ref[...] = (o * lax.rsqrt(ms + NORM_EPS) * hnw_ref[...] * _sigmoid(P[:, 3 * BR:])).astype(BF)


def _step_sample(ps, states, lp, lb, consts, bt=8):
    prw, pret, pgdn, pab, phg = ps
    n = prw.shape[0]
    cosf, sins = _rope_tables(PAST_LEN + jnp.arange(1))
    params = [lp[nm] for nm in ("rw_mu", "rw_w0", "rw_w2", "rw_a0", "rw_a2", "rw_g2", "rw_k_k", "rw_k_a",
                                "rw_r_k", "rw_ln_w", "rw_ln_b")]
    params += [cosf, sins, lp["gdn_conv_w"], lp["gdn_a_log"], lp["gdn_dt_bias"], lp["gdn_norm_w"], lb,
               lp["hg_norm_w"], consts["ones_bd"]]
    rows = lambda w: pl.BlockSpec((bt, w), lambda i: (i, 0))
    st4 = pl.BlockSpec((bt, NH, HD, HD), lambda i: (i, 0, 0, 0))
    cvs = pl.BlockSpec((bt, GDN_CONV - 1, CONV_CH), lambda i: (i, 0, 0))
    full = lambda t: pl.BlockSpec(t.shape, lambda i: (0,) * t.ndim)
    srw, ssh, sret, sgdn, scv, shg = states
    return pl.pallas_call(
        _step_body,
        grid=(n // bt,),
        in_specs=[rows(D_MODEL), rows(D_MODEL), rows(D_MODEL), rows(LANES), rows(D_MODEL),
                  st4, rows(D_MODEL), st4, st4, cvs, st4] + [full(t) for t in params],
        out_specs=[rows(BR)] * 4 + [st4, rows(D_MODEL), st4, st4, cvs, st4],
        out_shape=[jax.ShapeDtypeStruct((n, BR), BF)] * 4
        + [jax.ShapeDtypeStruct(srw.shape, F32), jax.ShapeDtypeStruct((n, D_MODEL), F32),
           jax.ShapeDtypeStruct(sret.shape, F32), jax.ShapeDtypeStruct(sgdn.shape, F32),
           jax.ShapeDtypeStruct(scv.shape, F32), jax.ShapeDtypeStruct(shg.shape, F32)],
        compiler_params=pltpu.CompilerParams(dimension_semantics=("arbitrary",),
                                             vmem_limit_bytes=VMEM_LIMIT),
        name="sample_step",
    )(prw, pret, pgdn, pab, phg, srw, ssh, sret, sgdn, scv, shg, *params)


def _constants():
    d = jnp.arange(BR)
    ones_bd = ((d[:, None] // HD) == (d[None, :] // HD)).astype(BF)
    consts = {"ones_bd": ones_bd, "eye": jnp.eye(HD, dtype=BF)}
    for nc in (1, 2, 4):
        i = jnp.arange(nc * CHUNK)
        tri = ((i[:, None] >= i[None, :]) & ((i[:, None] // CHUNK) == (i[None, :] // CHUNK))).astype(BF)
        consts["tri%d" % nc] = tri
        consts["triu%d" % nc] = tri.T
    consts["tri"] = consts["tri1"]
    return consts


def _layer_params(l, a):
    row = lambda t: t[l].reshape(1, -1).astype(F32)
    o_ret, o_gdn = D_MODEL, 2 * D_MODEL
    o_ab = o_gdn + 4 * BR
    o_hg = o_ab + 2 * NH
    o_gate = o_hg + 4 * BR
    w_in = a["w_in"][l]
    wab = jnp.pad(w_in[:, o_ab:o_hg], ((0, 0), (0, LANES - 2 * NH)))
    pad_row = lambda t: jnp.pad(t[l].reshape(1, -1).astype(F32), ((0, 0), (0, LANES - NH)))
    return {
        "norm_mix_w": row(a["norm_mix_w"]), "norm_ffn_w": row(a["norm_ffn_w"]),
        "w_mix": tuple(w.astype(BF) for w in (w_in[:, :o_ret], w_in[:, o_ret:o_gdn], w_in[:, o_gdn:o_ab], wab,
                                               w_in[:, o_hg:o_gate])),
        "w_gate": w_in[:, o_gate:].astype(BF), "w_branch": a["w_branch"][l].astype(BF),
        "w_out": a["w_out"][l].astype(BF), "w_gate_up": a["w_gate_up"][l].astype(BF),
        "w_down": a["w_down"][l].astype(BF),
        "rw_mu": row(a["rw_mu"]), "rw_w0": row(a["rw_w0"]), "rw_w2": a["rw_w2"][l].astype(BF),
        "rw_a0": row(a["rw_a0"]), "rw_a2": a["rw_a2"][l].astype(BF), "rw_g2": a["rw_g2"][l].astype(BF),
        "rw_k_k": row(a["rw_k_k"]), "rw_k_a": row(a["rw_k_a"]), "rw_r_k": row(a["rw_r_k"]),
        "rw_ln_w": row(a["rw_ln_w"]), "rw_ln_b": row(a["rw_ln_b"]),
        "gdn_conv_w": a["gdn_conv_w"][l].astype(F32), "gdn_a_log": pad_row(a["gdn_a_log"]),
        "gdn_dt_bias": pad_row(a["gdn_dt_bias"]),
        "gdn_norm_w": jnp.tile(a["gdn_norm_w"][l].reshape(1, HD).astype(F32), (1, NH)),
        "hg_norm_w": row(a["hg_norm_w"]),
    }


def kernel(x_prompt, x_sample, c_prompt, c_sample, state_rwkv, state_rwkv_shift, state_ret, state_gdn, state_gdn_conv, state_hgrn, ada_w, ada_b, norm_mix_w, w_in, rw_mu, rw_w0, rw_w2, rw_a0, rw_a2, rw_g2, rw_k_k, rw_k_a, rw_r_k, rw_ln_w, rw_ln_b, gdn_conv_w, gdn_a_log, gdn_dt_bias, gdn_norm_w, hg_lb_logits, hg_norm_w, w_branch, w_out, norm_ffn_w, w_gate_up, w_down, final_norm_w):
    a = dict(norm_mix_w=norm_mix_w, w_in=w_in, rw_mu=rw_mu, rw_w0=rw_w0, rw_w2=rw_w2, rw_a0=rw_a0, rw_a2=rw_a2,
             rw_g2=rw_g2, rw_k_k=rw_k_k, rw_k_a=rw_k_a, rw_r_k=rw_r_k, rw_ln_w=rw_ln_w, rw_ln_b=rw_ln_b,
             gdn_conv_w=gdn_conv_w, gdn_a_log=gdn_a_log, gdn_dt_bias=gdn_dt_bias, gdn_norm_w=gdn_norm_w,
             hg_norm_w=hg_norm_w, w_branch=w_branch, w_out=w_out, norm_ffn_w=norm_ffn_w, w_gate_up=w_gate_up,
             w_down=w_down)
    depth = ada_w.shape[0]
    Bp, Tp, _ = x_prompt.shape
    Bs = x_sample.shape[0]
    assert x_sample.shape[1] == 1 and Tp % CHUNK == 0
    tm = 256 if Tp % 256 == 0 else CHUNK
    consts = _constants()
    mod = _ada_mod(jnp.concatenate([c_prompt, c_sample], axis=0), ada_w.astype(F32), ada_b.astype(F32))
    lbs = _lower_bounds(hg_lb_logits)
    fw = final_norm_w.reshape(1, D_MODEL).astype(F32)

    xp = x_prompt
    xs = x_sample.reshape(1, Bs, D_MODEL)
    new_p = [[] for _ in range(6)]
    new_s = [[] for _ in range(6)]
    for l in range(depth):
        lp = _layer_params(l, a)
        lb = lbs[l:l + 1]
        mod_p = mod[l, :Bp].reshape(Bp, 1, -1)
        mod_s = mod[l, Bp:].reshape(1, Bs, -1)
        final = l == depth - 1

        h, prw, pret, pgdn, pab, phg = _in_proj(xp, mod_p, 1, tm, lp["norm_mix_w"], lp["w_mix"])
        y_rw, s_rw, s_sh = _rw_prompt(prw, lp, consts)
        y_ret, s_ret = _ret_prompt(pret, consts)
        y_gdn, s_gdn, s_cv = _gdn_prompt(pgdn, pab, lp["gdn_conv_w"], lp["gdn_a_log"], lp["gdn_dt_bias"],
                                         lp["gdn_norm_w"], consts)
        y_hg, s_hg = _hg_prompt(phg, lb, lp["hg_norm_w"], consts)
        xp = _merge(xp, h, (y_rw, y_ret, y_gdn, y_hg), mod_p, 1, tm, lp["w_gate"], lp["w_branch"], lp["w_out"])
        xp = _ffn(xp, mod_p, 1, tm, lp["norm_ffn_w"], lp["w_gate_up"], lp["w_down"], fw, final)
        for lst, s in zip(new_p, (s_rw, s_sh, s_ret, s_gdn, s_cv, s_hg)):
            lst.append(s)

        h, prw, pret, pgdn, pab, phg = _in_proj(xs, mod_s, Bs, Bs, lp["norm_mix_w"], lp["w_mix"])
        states = (state_rwkv[l], state_rwkv_shift[l].reshape(Bs, -1), state_ret[l], state_gdn[l],
                  state_gdn_conv[l], state_hgrn[l])
        outs = _step_sample(tuple(t[0] for t in (prw, pret, pgdn, pab, phg)), states, lp, lb, consts)
        ys = tuple(t[None] for t in outs[:4])
        xs = _merge(xs, h, ys, mod_s, Bs, Bs, lp["w_gate"], lp["w_branch"], lp["w_out"])
        xs = _ffn(xs, mod_s, Bs, Bs, lp["norm_ffn_w"], lp["w_gate_up"], lp["w_down"], fw, final)
        n_rw, n_sh, n_ret, n_gdn, n_cv, n_hg = outs[4:]
        for lst, s in zip(new_s, (n_rw, n_sh.reshape(Bs, 1, -1), n_ret, n_gdn, n_cv, n_hg)):
            lst.append(s)

    y_prompt = xp
    y_sample = xs.reshape(Bs, 1, D_MODEL)
    return (y_prompt, y_sample) + tuple(jnp.stack(t) for t in new_p) + tuple(jnp.stack(t) for t in new_s)
```

```python
import functools
import math

import jax
import jax.numpy as jnp
from jax import lax
from jax.experimental import pallas as pl
from jax.experimental.pallas import tpu as pltpu

F32 = jnp.float32
BF = jnp.bfloat16

D_MODEL = 1024
DEPTH = 4
PAST_LEN = 16384
BR = 256
HD = 64
NH = 4
RW_LORA_W = 64
RW_LORA_A = 64
RW_GN_EPS = 64e-5
ROPE_BASE = 10000.0
GDN_CONV = 4
CONV_CH = 3 * BR
D_FF = 2816
NORM_EPS = 1e-6
CHUNK = 64
SUB = 16
LANES = 128
VMEM_LIMIT = 56 * 1024 * 1024

NN = ((1,), (0,))
NT = ((1,), (1,))
TN = ((0,), (0,))


def _mm(a, b, dims=NN):
    return lax.dot_general(a.astype(BF), b.astype(BF), (dims, ((), ())), preferred_element_type=F32)


def _split2(a):
    hi = a.astype(BF)
    lo = (a - hi.astype(F32)).astype(BF)
    return hi, lo


def _split3(a):
    hi = a.astype(BF)
    r = a - hi.astype(F32)
    mid = r.astype(BF)
    lo = (r - mid.astype(F32)).astype(BF)
    return hi, mid, lo


def _mm_exact_rhs(a, b, dims=NN, parts=3):
    ps = _split3(a) if parts == 3 else _split2(a)
    out = _mm(ps[0], b, dims)
    for p in ps[1:]:
        out = out + _mm(p, b, dims)
    return out


def _mm_exact_lhs(a, b, dims=NN, parts=3):
    ps = _split3(b) if parts == 3 else _split2(b)
    out = _mm(a, ps[0], dims)
    for p in ps[1:]:
        out = out + _mm(a, p, dims)
    return out


def _sigmoid(x):
    return jax.nn.sigmoid(x)


def _silu(x):
    return x * jax.nn.sigmoid(x)


def _softplus(x):
    return jnp.maximum(x, 0.0) + jnp.log1p(jnp.exp(-jnp.abs(x)))


def _head_sum(x, ones_bd):
    return _mm(x, ones_bd)


def _tri_inverse(Ls, row, col):
    c = Ls[0].shape[0]
    eye = (row == col).astype(F32)
    bd = (row // SUB) == (col // SUB)
    Ps = [jnp.where(bd, L, 0.0) for L in Ls]
    Ds = [eye - P for P in Ps]
    for _ in range(int(math.log2(SUB)) - 1):
        Ps = [_mm(P, P) for P in Ps]
        Ds = [Dm + _mm(Dm, P) for Dm, P in zip(Ds, Ps)]
    size = SUB
    while size < c:
        off = ((row // (2 * size)) == (col // (2 * size))) & ((row // size) != (col // size))
        DCs = [_mm(Dm, jnp.where(off, L, 0.0)) for Dm, L in zip(Ds, Ls)]
        Ds = [Dm - _mm(DC, Dm) for Dm, DC in zip(Ds, DCs)]
        size *= 2
    return Ds


def _mod_body(c_ref, w_ref, b_ref, o_ref):
    o_ref[0] = _mm(_silu(c_ref[...]), w_ref[0]) + b_ref[0]


def _ada_mod(c_all, ada_w, ada_b):
    n = c_all.shape[0]
    ncol = ada_w.shape[2] // D_MODEL
    return pl.pallas_call(
        _mod_body,
        grid=(DEPTH, ncol),
        in_specs=[
            pl.BlockSpec((n, D_MODEL), lambda l, j: (0, 0)),
            pl.BlockSpec((1, D_MODEL, D_MODEL), lambda l, j: (l, 0, j)),
            pl.BlockSpec((1, 1, D_MODEL), lambda l, j: (l, 0, j)),
        ],
        out_specs=pl.BlockSpec((1, n, D_MODEL), lambda l, j: (l, 0, j)),
        out_shape=jax.ShapeDtypeStruct((DEPTH, n, ncol * D_MODEL), F32),
        compiler_params=pltpu.CompilerParams(dimension_semantics=("arbitrary", "arbitrary"),
                                             vmem_limit_bytes=VMEM_LIMIT),
        name="ada_mod",
    )(c_all, ada_w, ada_b.reshape(DEPTH, 1, -1))


def _lb_body(x_ref, o_ref):
    x = x_ref[...]
    m = jnp.max(x, axis=0, keepdims=True)
    e = jnp.exp(x - m)
    p = e / jnp.sum(e, axis=0, keepdims=True)
    acc = jnp.zeros_like(p[0:1])
    rows = [acc]
    for l in range(1, DEPTH):
        acc = acc + p[l:l + 1]
        rows.append(acc)
    o_ref[...] = jnp.concatenate(rows, axis=0)


def _lower_bounds(logits):
    return pl.pallas_call(
        _lb_body,
        out_shape=jax.ShapeDtypeStruct(logits.shape, F32),
        name="hgrn_lower_bounds",
    )(logits.astype(F32))


def _mod_spec(rm, tm, col):
    if rm == 1:
        return pl.BlockSpec((1, 1, D_MODEL), lambda b, i: (b, 0, col))
    return pl.BlockSpec((1, tm, D_MODEL), lambda b, i: (b, i, col))


def _full_spec(shape):
    nd = len(shape)
    return pl.BlockSpec(shape, lambda b, i: (0,) * nd)


def _weight_spec(shape):
    nd = len(shape)
    return pl.BlockSpec(shape, lambda b, i: (0,) * nd, pipeline_mode=pl.Buffered(1))


def _row_spec(tm, width):
    return pl.BlockSpec((1, tm, width), lambda b, i: (b, i, 0))


def _modulated_norm(x, nw, sc, sh):
    ms = jnp.mean(x * x, axis=-1, keepdims=True)
    return (x * lax.rsqrt(ms + NORM_EPS) * nw) * (1.0 + sc) + sh


def _in_body(x_ref, sc_ref, sh_ref, nw_ref, wrw_ref, wret_ref, wgdn_ref, wab_ref, whg_ref,
             h_ref, prw_ref, pret_ref, pgdn_ref, pab_ref, phg_ref):
    h = _modulated_norm(x_ref[0], nw_ref[...], sc_ref[0], sh_ref[0]).astype(BF)
    h_ref[0] = h
    prw_ref[0] = _mm(h, wrw_ref[...])
    pret_ref[0] = _mm(h, wret_ref[...])
    pgdn_ref[0] = _mm(h, wgdn_ref[...])
    pab_ref[0] = _mm(h, wab_ref[...])
    phg_ref[0] = _mm(h, whg_ref[...])


def _in_proj(x, mod, rm, tm, nw, ws):
    G, R, _ = x.shape
    wrw, wret, wgdn, wab, whg = ws
    outs = [jax.ShapeDtypeStruct((G, R, D_MODEL), BF)] + [
        jax.ShapeDtypeStruct((G, R, w.shape[1]), F32) for w in ws]
    return pl.pallas_call(
        _in_body,
        grid=(G, R // tm),
        in_specs=[_row_spec(tm, D_MODEL), _mod_spec(rm, tm, 1), _mod_spec(rm, tm, 0),
                  _full_spec((1, D_MODEL))] + [_weight_spec(w.shape) for w in ws],
        out_specs=[_row_spec(tm, D_MODEL)] + [_row_spec(tm, w.shape[1]) for w in ws],
        out_shape=outs,
        compiler_params=pltpu.CompilerParams(dimension_semantics=("arbitrary", "arbitrary"),
                                             vmem_limit_bytes=VMEM_LIMIT),
        name="in_proj",
    )(x, mod, mod, nw, wrw, wret, wgdn, wab, whg)


def _merge_body(x_ref, h_ref, yrw_ref, yret_ref, ygdn_ref, yhg_ref, g1_ref, wg_ref, wb_ref, wo_ref,
                o_ref):
    h = h_ref[0]
    merged = None
    for n, y_ref in enumerate((yrw_ref, yret_ref, ygdn_ref, yhg_ref)):
        gate = _sigmoid(_mm(h, wg_ref[:, n * D_MODEL:(n + 1) * D_MODEL]))
        term = gate * _mm(y_ref[0], wb_ref[n])
        merged = term if merged is None else merged + term
    o_ref[0] = x_ref[0] + g1_ref[0] * _mm(merged, wo_ref[...])


def _merge(x, h, ys, mod, rm, tm, wg, wb, wo):
    G, R, _ = x.shape
    return pl.pallas_call(
        _merge_body,
        grid=(G, R // tm),
        in_specs=[_row_spec(tm, D_MODEL), _row_spec(tm, D_MODEL)] + [_row_spec(tm, BR)] * 4
        + [_mod_spec(rm, tm, 2), _weight_spec(wg.shape), _weight_spec(wb.shape), _weight_spec(wo.shape)],
        out_specs=_row_spec(tm, D_MODEL),
        out_shape=jax.ShapeDtypeStruct((G, R, D_MODEL), F32),
        compiler_params=pltpu.CompilerParams(dimension_semantics=("arbitrary", "arbitrary"),
                                             vmem_limit_bytes=VMEM_LIMIT),
        name="merge_out",
    )(x, h, *ys, mod, wg, wb, wo)


def _ffn_body(final, x_ref, sc_ref, sh_ref, g2_ref, nw_ref, wgu_ref, wd_ref, fw_ref, o_ref):
    x = x_ref[0]
    h = _modulated_norm(x, nw_ref[...], sc_ref[0], sh_ref[0]).astype(BF)
    gate = _mm(h, wgu_ref[:, :D_FF])
    up = _mm(h, wgu_ref[:, D_FF:])
    x = x + g2_ref[0] * _mm(_silu(gate) * up, wd_ref[...])
    if final:
        ms = jnp.mean(x * x, axis=-1, keepdims=True)
        x = x * lax.rsqrt(ms + NORM_EPS) * fw_ref[...]
    o_ref[0] = x


def _ffn(x, mod, rm, tm, nw, wgu, wd, fw, final):
    G, R, _ = x.shape
    return pl.pallas_call(
        functools.partial(_ffn_body, final),
        grid=(G, R // tm),
        in_specs=[_row_spec(tm, D_MODEL), _mod_spec(rm, tm, 4), _mod_spec(rm, tm, 3), _mod_spec(rm, tm, 5),
                  _full_spec((1, D_MODEL)), _weight_spec(wgu.shape), _weight_spec(wd.shape),
                  _full_spec((1, D_MODEL))],
        out_specs=_row_spec(tm, D_MODEL),
        out_shape=jax.ShapeDtypeStruct((G, R, D_MODEL), F32),
        compiler_params=pltpu.CompilerParams(dimension_semantics=("arbitrary", "arbitrary"),
                                             vmem_limit_bytes=VMEM_LIMIT),
        name="ffn",
    )(x, mod, mod, mod, nw, wgu, wd, fw)


def _chunk_spec(c, width):
    return pl.BlockSpec((1, c, width), lambda b, i: (b, i, 0))


def _state_spec():
    return pl.BlockSpec((1, NH, HD, HD), lambda b, i: (b, 0, 0, 0))


def _iota2(shape):
    return (lax.broadcasted_iota(jnp.int32, shape, 0), lax.broadcasted_iota(jnp.int32, shape, 1))


def _rope(t, cosf, sins):
    lane = lax.broadcasted_iota(jnp.int32, t.shape, 1)
    width = t.shape[1]
    rot = jnp.where((lane % HD) < HD // 2, pltpu.roll(t, width - HD // 2, 1), pltpu.roll(t, HD // 2, 1))
    return t * cosf + rot * sins


def _ret_body(nc, p_ref, cos_ref, sin_ref, qin_ref, kout_ref, dmat_ref, cdec_ref, ones_ref, y_ref, s_ref):
    c = CHUNK

    @pl.when(pl.program_id(1) == 0)
    def _():
        s_ref[...] = jnp.zeros_like(s_ref)

    P = p_ref[0]
    cosf, sins = cos_ref[...], sin_ref[...]
    q = _rope(P[:, :BR], cosf, sins)
    k = _rope(P[:, BR:2 * BR], cosf, sins) * HD ** -0.5
    v = P[:, 2 * BR:3 * BR]
    g = P[:, 3 * BR:]
    qs = q * qin_ref[...]
    ko = k * kout_ref[...]
    items = [(ci, h) for ci in range(nc) for h in range(NH)]

    def sl(t, ci, h):
        return t[ci * c:(ci + 1) * c, h * HD:(h + 1) * HD]

    ss = [_mm(sl(q, ci, h), sl(k, ci, h), NT) * dmat_ref[h] for ci, h in items]
    oi = [_mm(s, sl(v, ci, h)) for s, (ci, h) in zip(ss, items)]
    kv = [_mm(sl(ko, ci, h), sl(v, ci, h), TN) for ci, h in items]
    outs = [[None] * NH for _ in range(nc)]
    for idx, (ci, h) in enumerate(items):
        S = s_ref[0, h]
        outs[ci][h] = oi[idx] + _mm(sl(qs, ci, h), S)
        s_ref[0, h] = S * cdec_ref[h] + kv[idx]
    o = jnp.concatenate([jnp.concatenate(r, axis=1) for r in outs], axis=0)
    ms = _head_sum(o * o, ones_ref[...]) * (1.0 / HD)
    y_ref[0] = (o * lax.rsqrt(ms + NORM_EPS) * _silu(g)).astype(BF)


def _ret_tables(c):
    lg = jnp.log1p(-jnp.exp2(-5.0 - jnp.arange(NH, dtype=F32)))[:, None]
    idx = jnp.arange(c, dtype=F32)
    diff = idx[:, None] - idx[None, :]
    dmat = jnp.where(diff >= 0, jnp.exp(lg[:, :, None] * jnp.maximum(diff, 0.0)), 0.0)
    q_in = jnp.exp(lg * (idx + 1.0))
    k_out = jnp.exp(lg * (c - 1.0 - idx))
    c_dec = jnp.exp(lg * c)
    expand = lambda t: jnp.repeat(t.T, HD, axis=1)
    return dmat, expand(q_in), expand(k_out), jnp.broadcast_to(c_dec[:, :, None], (NH, HD, HD))


def _rope_tables(pos):
    half = HD // 2
    freqs = ROPE_BASE ** (-jnp.arange(half, dtype=F32) / half)
    ang = pos.astype(F32)[:, None] * freqs
    cos, sin = jnp.cos(ang), jnp.sin(ang)
    cosf = jnp.tile(jnp.concatenate([cos, cos], axis=1), (1, NH))
    sins = jnp.tile(jnp.concatenate([-sin, sin], axis=1), (1, NH))
    return cosf, sins


def _ret_prompt(p, consts):
    B, T, _ = p.shape
    c = CHUNK
    nc = _chunks_per_step(T)
    tb = nc * c
    cosf, sins = _rope_tables(jnp.arange(T))
    dmat, q_in, k_out, c_dec = _ret_tables(c)
    q_in, k_out = jnp.tile(q_in, (nc, 1)), jnp.tile(k_out, (nc, 1))
    return pl.pallas_call(
        functools.partial(_ret_body, nc),
        grid=(B, T // tb),
        in_specs=[_chunk_spec(tb, D_MODEL),
                  pl.BlockSpec((tb, BR), lambda b, i: (i, 0)), pl.BlockSpec((tb, BR), lambda b, i: (i, 0)),
                  _full_spec((tb, BR)), _full_spec((tb, BR)), _full_spec((NH, c, c)), _full_spec((NH, HD, HD)),
                  _full_spec((BR, BR))],
        out_specs=[_chunk_spec(tb, BR), _state_spec()],
        out_shape=[jax.ShapeDtypeStruct((B, T, BR), BF), jax.ShapeDtypeStruct((B, NH, HD, HD), F32)],
        compiler_params=pltpu.CompilerParams(dimension_semantics=("arbitrary", "arbitrary"),
                                             vmem_limit_bytes=VMEM_LIMIT),
        name="ret_prompt",
    )(p, cosf, sins, q_in, k_out, dmat, c_dec, consts["ones_bd"])


def _chunks_per_step(T):
    for nc in (4, 2):
        if T % (nc * CHUNK) == 0:
            return nc
    return 1


def _gdn_body(nc, pg_ref, pab_ref, cw_ref, alog_ref, dtb_ref, nw_ref, tri_ref, triu_ref, ones_ref,
              y_ref, s_ref, cv_ref, xext):
    c = CHUNK
    tb = nc * c

    @pl.when(pl.program_id(1) == 0)
    def _():
        s_ref[...] = jnp.zeros_like(s_ref)
        xext[pl.ds(0, 8), :] = jnp.zeros((8, CONV_CH), F32)

    P = pg_ref[0]
    X = P[:, :CONV_CH]
    xext[pl.ds(8, tb), :] = X
    cw = cw_ref[...]
    conv = xext[pl.ds(8 - 3, tb), :] * cw[0:1]
    conv = conv + xext[pl.ds(8 - 2, tb), :] * cw[1:2]
    conv = conv + xext[pl.ds(8 - 1, tb), :] * cw[2:3]
    conv = conv + X * cw[3:4]
    xext[pl.ds(0, 8), :] = X[tb - 8:tb]
    cv_ref[0] = X[tb - (GDN_CONV - 1):tb]
    conv = _silu(conv)
    ones_bd = ones_ref[...]
    qn, kn, v = conv[:, :BR], conv[:, BR:2 * BR], conv[:, 2 * BR:]
    q = qn * lax.rsqrt(_head_sum(qn * qn, ones_bd) + 1e-6) * HD ** -0.5
    k = kn * lax.rsqrt(_head_sum(kn * kn, ones_bd) + 1e-6)
    gate = P[:, CONV_CH:]

    ab = pab_ref[0]
    glog = -jnp.exp(alog_ref[...]) * _softplus(ab + dtb_ref[...])
    beta = _sigmoid(ab)
    Gc = _mm_exact_lhs(tri_ref[...], glog, NN)
    Gr = _mm_exact_rhs(glog, triu_ref[...], TN)
    eG = jnp.exp(Gc)
    row, col = _iota2((c, c))
    incl = row >= col
    strict = row > col
    items = [(ci, h) for ci in range(nc) for h in range(NH)]

    def sl(t, ci, h):
        return t[ci * c:(ci + 1) * c, h * HD:(h + 1) * HD]

    KQ = [_mm(jnp.concatenate([sl(k, ci, h), sl(q, ci, h)], axis=0), sl(k, ci, h), NT) for ci, h in items]
    Ls, qks, rhss, kouts, cdecs = [], [], [], [], []
    for (ci, h), kq in zip(items, KQ):
        rows = slice(ci * c, (ci + 1) * c)
        gc = Gc[rows, h:h + 1]
        gr = Gr[h:h + 1, rows]
        bcol = beta[rows, NH + h:NH + h + 1]
        gam = jnp.where(incl, jnp.exp(jnp.where(incl, gc - gr, 0.0)), 0.0)
        Ls.append(jnp.where(strict, bcol * gam * kq[:c], 0.0))
        qks.append(kq[c:] * gam)
        rhss.append(jnp.concatenate([sl(v, ci, h) * bcol, sl(k, ci, h) * (bcol * eG[rows, h:h + 1])], axis=1))
        gend = gc[c - 1:c, :]
        kouts.append(sl(k, ci, h) * jnp.exp(gend - gc))
        cdecs.append(jnp.exp(gend))
    Ts = _tri_inverse(Ls, row, col)
    sols = [_mm(T, rhs) for T, rhs in zip(Ts, rhss)]
    QS = [_mm(qk, sol) for qk, sol in zip(qks, sols)]
    KS = [_mm(kout, sol, TN) for kout, sol in zip(kouts, sols)]
    outs = [[None] * NH for _ in range(nc)]
    for idx, (ci, h) in enumerate(items):
        rows = slice(ci * c, (ci + 1) * c)
        QW = sl(q, ci, h) * eG[rows, h:h + 1] - QS[idx][:, HD:]
        S = s_ref[0, h]
        Z = _mm(jnp.concatenate([KS[idx][:, HD:], QW], axis=0), S)
        s_ref[0, h] = S * cdecs[idx] - Z[:HD] + KS[idx][:, :HD]
        outs[ci][h] = Z[HD:] + QS[idx][:, :HD]
    o = jnp.concatenate([jnp.concatenate(r, axis=1) for r in outs], axis=0)
    ms = _head_sum(o * o, ones_bd) * (1.0 / HD)
    y_ref[0] = (o * lax.rsqrt(ms + NORM_EPS) * nw_ref[...] * _silu(gate)).astype(BF)


def _gdn_prompt(pg, pab, cw, alog, dtb, nw, consts):
    B, T, _ = pg.shape
    nc = _chunks_per_step(T)
    tb = nc * CHUNK
    return pl.pallas_call(
        functools.partial(_gdn_body, nc),
        grid=(B, T // tb),
        in_specs=[_chunk_spec(tb, D_MODEL), _chunk_spec(tb, LANES), _full_spec((GDN_CONV, CONV_CH)),
                  _full_spec((1, LANES)), _full_spec((1, LANES)), _full_spec((1, BR)),
                  _full_spec((tb, tb)), _full_spec((tb, tb)), _full_spec((BR, BR))],
        out_specs=[_chunk_spec(tb, BR), _state_spec(),
                   pl.BlockSpec((1, GDN_CONV - 1, CONV_CH), lambda b, i: (b, 0, 0))],
        out_shape=[jax.ShapeDtypeStruct((B, T, BR), BF), jax.ShapeDtypeStruct((B, NH, HD, HD), F32),
                   jax.ShapeDtypeStruct((B, GDN_CONV - 1, CONV_CH), F32)],
        scratch_shapes=[pltpu.VMEM((tb + 8, CONV_CH), F32)],
        compiler_params=pltpu.CompilerParams(dimension_semantics=("arbitrary", "arbitrary"),
                                             vmem_limit_bytes=VMEM_LIMIT),
        name="gdn_prompt",
    )(pg, pab, cw, alog, dtb, nw, consts["tri%d" % nc], consts["triu%d" % nc], consts["ones_bd"])


def _hgrn_gates(xq, xf, lb):
    q = _silu(xq)
    ls = jnp.minimum(xf, 0.0) - jnp.log1p(jnp.exp(-jnp.abs(xf)))
    pos = lb > 0.0
    a = jnp.log(jnp.where(pos, lb, 1.0))
    b = jnp.log1p(-lb) + ls
    lae = jnp.maximum(a, b) + jnp.log1p(jnp.exp(-jnp.abs(a - b)))
    log_f = jnp.where(pos, lae, ls)
    k = (1.0 - lb) * _sigmoid(-xf)
    return q, log_f, k


def _hg_body(nc, p_ref, lb_ref, nw_ref, tri_ref, ones_ref, eye_ref, y_ref, s_ref, st, tbuf, sbuf):
    c = CHUNK
    tb = nc * c
    nb = tb // SUB
    half = SUB // 2
    gi = pl.program_id(1)

    @pl.when(gi == 0)
    def _():
        st[...] = jnp.zeros_like(st)

    P = p_ref[0]
    q, log_f, k = _hgrn_gates(P[:, :BR], P[:, BR:2 * BR], lb_ref[...])
    v = P[:, 2 * BR:3 * BR]
    og = P[:, 3 * BR:]
    ones_bd = ones_ref[...]
    G = _mm_exact_lhs(tri_ref[...], log_f, NN)

    G3, q3, k3, v3 = (t.reshape(nb, SUB, BR) for t in (G, q, k, v))
    o_lo = jnp.zeros((nb, half, BR), F32)
    o_hi = jnp.zeros((nb, half, BR), F32)
    offs, total = [], 0
    for jj in range(SUB):
        offs.append(total)
        total += nb * (SUB if jj < half else half)
    for jj in range(SUB):
        i0 = 0 if jj < half else half
        rowi = lax.broadcasted_iota(jnp.int32, (nb, SUB - i0, BR), 1) + i0
        e = jnp.exp(jnp.minimum(G3[:, i0:, :] - G3[:, jj:jj + 1, :], 0.0))
        t = jnp.where(rowi >= jj, q3[:, i0:, :] * e * k3[:, jj:jj + 1, :], 0.0)
        tbuf[pl.ds(offs[jj], nb * (SUB - i0)), :] = t.reshape(nb * (SUB - i0), BR).astype(BF)
    sbuf[...] = jnp.dot(tbuf[...], ones_bd, preferred_element_type=F32)
    for jj in range(SUB):
        i0 = 0 if jj < half else half
        s = sbuf[pl.ds(offs[jj], nb * (SUB - i0)), :].reshape(nb, SUB - i0, BR)
        contrib = s * v3[:, jj:jj + 1, :]
        if i0 == 0:
            o_lo = o_lo + contrib[:, :half]
            o_hi = o_hi + contrib[:, half:]
        else:
            o_hi = o_hi + contrib
    o = jnp.concatenate([o_lo, o_hi], axis=1).reshape(tb, BR)

    pairs = [(ci, bi) for ci in range(nc) for bi in range(1, c // SUB)]
    qts, kts = [], []
    for ci, bi in pairs:
        r0 = ci * c
        ref = G[r0 + bi * SUB - 1:r0 + bi * SUB, :]
        rows_i = slice(r0 + bi * SUB, r0 + (bi + 1) * SUB)
        rows_j = slice(r0, r0 + bi * SUB)
        qts.append(q[rows_i] * jnp.exp(jnp.minimum(G[rows_i] - ref, 0.0)))
        kts.append(k[rows_j] * jnp.exp(jnp.minimum(ref - G[rows_j], 0.0)))
    wts = [[_mm(qt[:, h * HD:(h + 1) * HD], kt[:, h * HD:(h + 1) * HD], NT) for h in range(NH)]
           for qt, kt in zip(qts, kts)]
    offs = [[_mm(w[h], v[ci * c:ci * c + bi * SUB, h * HD:(h + 1) * HD]) for h in range(NH)]
            for (ci, bi), w in zip(pairs, wts)]
    off_rows = []
    idx = 0
    for ci in range(nc):
        off_rows.append(jnp.zeros((SUB, BR), F32))
        for bi in range(1, c // SUB):
            off_rows.append(jnp.concatenate(offs[idx], axis=1))
            idx += 1
    o = o + jnp.concatenate(off_rows, axis=0)

    qg = q * jnp.exp(G)
    GE = jnp.concatenate([jnp.broadcast_to(G[(ci + 1) * c - 1:(ci + 1) * c], (c, BR)) for ci in range(nc)], axis=0)
    kd = k * jnp.exp(GE - G)
    items = [(ci, h) for ci in range(nc) for h in range(NH)]

    def sl(t, ci, h):
        return t[ci * c:(ci + 1) * c, h * HD:(h + 1) * HD]

    vk = [_mm(sl(v, ci, h), sl(kd, ci, h), TN) for ci, h in items]
    outs = [[None] * NH for _ in range(nc)]
    for idx, (ci, h) in enumerate(items):
        ST = st[h]
        outs[ci][h] = _mm(sl(qg, ci, h), ST, NT)
        st[h] = ST * jnp.exp(G[(ci + 1) * c - 1:(ci + 1) * c, h * HD:(h + 1) * HD]) + vk[idx]
    o = o + jnp.concatenate([jnp.concatenate(r, axis=1) for r in outs], axis=0)
    ms = _head_sum(o * o, ones_bd) * (1.0 / HD)
    y_ref[0] = (o * lax.rsqrt(ms + NORM_EPS) * nw_ref[...] * _sigmoid(og)).astype(BF)

    @pl.when(gi == pl.num_programs(1) - 1)
    def _():
        for h in range(NH):
            s_ref[0, h] = _mm_exact_lhs(eye_ref[...], st[h], NT)


def _hg_prompt(p, lb, nw, consts):
    B, T, _ = p.shape
    nc = _chunks_per_step(T)
    tb = nc * CHUNK
    return pl.pallas_call(
        functools.partial(_hg_body, nc),
        grid=(B, T // tb),
        in_specs=[_chunk_spec(tb, D_MODEL), _full_spec((1, BR)), _full_spec((1, BR)), _full_spec((tb, tb)),
                  _full_spec((BR, BR)), _full_spec((HD, HD))],
        out_specs=[_chunk_spec(tb, BR), _state_spec()],
        out_shape=[jax.ShapeDtypeStruct((B, T, BR), BF), jax.ShapeDtypeStruct((B, NH, HD, HD), F32)],
        scratch_shapes=[pltpu.VMEM((NH, HD, HD), F32),
                        pltpu.VMEM((tb // SUB * (SUB + SUB // 2) * (SUB // 2), BR), BF),
                        pltpu.VMEM((tb // SUB * (SUB + SUB // 2) * (SUB // 2), BR), F32)],
        compiler_params=pltpu.CompilerParams(dimension_semantics=("arbitrary", "arbitrary"),
                                             vmem_limit_bytes=VMEM_LIMIT),
        name="hgrn_prompt",
    )(p, lb, nw, consts["tri%d" % nc], consts["ones_bd"], consts["eye"])


def _rw_features(P, prev, mu, w0, w2, a0, a2, g2, kkw, kaw, ones_bd):
    Pm = P + mu * (prev - P)
    r, kx, vx = Pm[:, :BR], Pm[:, BR:2 * BR], Pm[:, 2 * BR:3 * BR]
    o4 = 3 * BR + RW_LORA_W
    o5 = o4 + RW_LORA_A
    wl, al, gl = Pm[:, 3 * BR:o4], Pm[:, o4:o5], Pm[:, o5:]
    zw = w0 + _mm(jnp.tanh(wl), w2)
    lw = -jnp.exp(-_softplus(-zw) - 0.5)
    a = _sigmoid(a0 + _mm(al, a2))
    g = _mm(_sigmoid(gl), g2)
    kks = kx * kkw
    kk = kks * lax.rsqrt(_head_sum(kks * kks, ones_bd) + 1e-6)
    k = kx * (1.0 + (a - 1.0) * kaw)
    return r, lw, k, vx, -kk, kk * a, g


def _rw_output(o, r, k, vx, g, rkw, lnw, lnb, ones_bd):
    mean = _head_sum(o, ones_bd) * (1.0 / HD)
    cen = o - mean
    var = _head_sum(cen * cen, ones_bd) * (1.0 / HD)
    on = cen * lax.rsqrt(var + RW_GN_EPS) * lnw + lnb
    bonus = _head_sum(r * k * rkw, ones_bd) * vx
    return (on + bonus) * g


def _rw_body(nc, p_ref, mu_ref, w0_ref, w2_ref, a0_ref, a2_ref, g2_ref, kkw_ref, kaw_ref, rkw_ref,
             lnw_ref, lnb_ref, tri_ref, ones_ref, eye_ref, y_ref, s_ref, sh_ref, st, prev):
    c = CHUNK
    tb = nc * c
    gi = pl.program_id(1)

    @pl.when(gi == 0)
    def _():
        st[...] = jnp.zeros_like(st)
        prev[...] = jnp.zeros_like(prev)

    P = p_ref[0]
    rowp = lax.broadcasted_iota(jnp.int32, P.shape, 0)
    shifted = jnp.where(rowp == 0, prev[0:1, :], pltpu.roll(P, 1, 0))
    prev[0:1, :] = P[tb - 1:tb, :]
    sh_ref[0] = P[tb - 1:tb, :]
    ones_bd = ones_ref[...]
    r, lw, k, vx, av, bv, g = _rw_features(P, shifted, mu_ref[...], w0_ref[...], w2_ref[...], a0_ref[...],
                                           a2_ref[...], g2_ref[...], kkw_ref[...], kaw_ref[...], ones_bd)
    LG = _mm_exact_lhs(tri_ref[...], lw, NN)
    LGE = jnp.concatenate([jnp.broadcast_to(LG[(ci + 1) * c - 1:(ci + 1) * c], (c, BR)) for ci in range(nc)],
                          axis=0)
    rt = r * jnp.exp(LG)
    at = av * jnp.exp(LG - lw)
    einv = jnp.exp(-LG)
    kh_ = k * einv
    bh_ = bv * einv
    eo = jnp.exp(LGE - LG)
    kout = k * eo
    bout = bv * eo
    row, col = _iota2((c, c))
    incl = row >= col
    strict = row > col
    items = [(ci, h) for ci in range(nc) for h in range(NH)]

    def sl(t, ci, h):
        return t[ci * c:(ci + 1) * c, h * HD:(h + 1) * HD]

    Ms = [_mm(jnp.concatenate([sl(at, ci, h), sl(rt, ci, h)], axis=0),
              jnp.concatenate([sl(bh_, ci, h), sl(kh_, ci, h)], axis=0), NT) for ci, h in items]
    Aabs = [jnp.where(strict, M[:c, :c], 0.0) for M in Ms]
    Arbs = [jnp.where(incl, M[c:, :c], 0.0) for M in Ms]
    AKs = [jnp.concatenate([jnp.where(strict, M[:c, c:], 0.0), jnp.where(incl, M[c:, c:], 0.0)], axis=0)
           for M in Ms]
    AVs = [_mm(AK, sl(vx, ci, h)) for AK, (ci, h) in zip(AKs, items)]
    Ts = _tri_inverse([-A for A in Aabs], row, col)
    WTs = [_mm(T, jnp.concatenate([AV[:c], sl(at, ci, h)], axis=1))
           for T, AV, (ci, h) in zip(Ts, AVs, items)]
    AWs = [_mm(Arb, WT) for Arb, WT in zip(Arbs, WTs)]
    CBs = [_mm(WT, sl(bout, ci, h), TN) for WT, (ci, h) in zip(WTs, items)]
    VKs = [_mm(sl(vx, ci, h), sl(kout, ci, h), TN) for ci, h in items]
    outs = [[None] * NH for _ in range(nc)]
    for idx, (ci, h) in enumerate(items):
        dec = jnp.exp(LG[(ci + 1) * c - 1:(ci + 1) * c, h * HD:(h + 1) * HD])
        RQ = sl(rt, ci, h) + AWs[idx][:, HD:]
        ST = st[h]
        outs[ci][h] = _mm(RQ, ST, NT) + AWs[idx][:, :HD] + AVs[idx][c:]
        st[h] = ST * dec + _mm(ST, CBs[idx][HD:]) + CBs[idx][:HD] + VKs[idx]
    o = jnp.concatenate([jnp.concatenate(rw, axis=1) for rw in outs], axis=0)
    y_ref[0] = _rw_output(o, r, k, vx, g, rkw_ref[...], lnw_ref[...], lnb_ref[...], ones_bd).astype(BF)

    @pl.when(gi == pl.num_programs(1) - 1)
    def _():
        for h in range(NH):
            s_ref[0, h] = _mm_exact_lhs(eye_ref[...], st[h], NT)


def _rw_prompt(p, lp, consts):
    B, T, _ = p.shape
    nc = _chunks_per_step(T)
    tb = nc * CHUNK
    params = [lp[n] for n in ("rw_mu", "rw_w0", "rw_w2", "rw_a0", "rw_a2", "rw_g2", "rw_k_k", "rw_k_a",
                              "rw_r_k", "rw_ln_w", "rw_ln_b")]
    return pl.pallas_call(
        functools.partial(_rw_body, nc),
        grid=(B, T // tb),
        in_specs=[_chunk_spec(tb, D_MODEL)] + [_full_spec(t.shape) for t in params]
        + [_full_spec((tb, tb)), _full_spec((BR, BR)), _full_spec((HD, HD))],
        out_specs=[_chunk_spec(tb, BR), _state_spec(),
                   pl.BlockSpec((1, 1, D_MODEL), lambda b, i: (b, 0, 0))],
        out_shape=[jax.ShapeDtypeStruct((B, T, BR), BF), jax.ShapeDtypeStruct((B, NH, HD, HD), F32),
                   jax.ShapeDtypeStruct((B, 1, D_MODEL), F32)],
        scratch_shapes=[pltpu.VMEM((NH, HD, HD), F32), pltpu.VMEM((8, D_MODEL), F32)],
        compiler_params=pltpu.CompilerParams(dimension_semantics=("arbitrary", "arbitrary"),
                                             vmem_limit_bytes=VMEM_LIMIT),
        name="rwkv_prompt",
    )(p, *params, consts["tri%d" % nc], consts["ones_bd"], consts["eye"])


def _col(x):
    return x[:, :, None]


def _rowv(x):
    return x[:, None, :]


def _step_body(prw_ref, pret_ref, pgdn_ref, pab_ref, phg_ref,
               srw_ref, ssh_ref, sret_ref, sgdn_ref, scv_ref, shg_ref,
               mu_ref, w0_ref, w2_ref, a0_ref, a2_ref, g2_ref, kkw_ref, kaw_ref, rkw_ref, lnw_ref, lnb_ref,
               cos_ref, sin_ref, cw_ref, alog_ref, dtb_ref, gnw_ref, lb_ref, hnw_ref, ones_ref,
               yrw_ref, yret_ref, ygdn_ref, yhg_ref,
               nrw_ref, nsh_ref, nret_ref, ngdn_ref, ncv_ref, nhg_ref):
    ones_bd = ones_ref[...]

    P = prw_ref[...]
    nsh_ref[...] = P
    r, lw, k, vx, av, bv, g = _rw_features(P, ssh_ref[...], mu_ref[...], w0_ref[...], w2_ref[...], a0_ref[...],
                                           a2_ref[...], g2_ref[...], kkw_ref[...], kaw_ref[...], ones_bd)
    w = jnp.exp(lw)
    outs = []
    for h in range(NH):
        sl = slice(h * HD, (h + 1) * HD)
        S = srw_ref[:, h]
        sa = jnp.sum(S * _col(av[:, sl]), axis=1, keepdims=True)
        S = _col(w[:, sl]) * S + _col(bv[:, sl]) * sa + _col(k[:, sl]) * _rowv(vx[:, sl])
        nrw_ref[:, h] = S
        outs.append(jnp.sum(S * _col(r[:, sl]), axis=1))
    o = jnp.concatenate(outs, axis=1)
    yrw_ref[...] = _rw_output(o, r, k, vx, g, rkw_ref[...], lnw_ref[...], lnb_ref[...], ones_bd).astype(BF)

    P = pret_ref[...]
    cosf, sins = cos_ref[...], sin_ref[...]
    q = _rope(P[:, :BR], cosf, sins)
    k = _rope(P[:, BR:2 * BR], cosf, sins) * HD ** -0.5
    v = P[:, 2 * BR:3 * BR]
    outs = []
    for h in range(NH):
        sl = slice(h * HD, (h + 1) * HD)
        S = sret_ref[:, h] * (1.0 - 2.0 ** (-5 - h)) + _col(k[:, sl]) * _rowv(v[:, sl])
        nret_ref[:, h] = S
        outs.append(jnp.sum(S * _col(q[:, sl]), axis=1))
    o = jnp.concatenate(outs, axis=1)
    ms = _head_sum(o * o, ones_bd) * (1.0 / HD)
    yret_ref[...] = (o * lax.rsqrt(ms + NORM_EPS) * _silu(P[:, 3 * BR:])).astype(BF)

    P = pgdn_ref[...]
    X = P[:, :CONV_CH]
    cw = cw_ref[...]
    conv = scv_ref[:, 0, :] * cw[0:1] + scv_ref[:, 1, :] * cw[1:2] + scv_ref[:, 2, :] * cw[2:3] + X * cw[3:4]
    ncv_ref[:, 0, :] = scv_ref[:, 1, :]
    ncv_ref[:, 1, :] = scv_ref[:, 2, :]
    ncv_ref[:, 2, :] = X
    conv = _silu(conv)
    qn, kn, v = conv[:, :BR], conv[:, BR:2 * BR], conv[:, 2 * BR:]
    q = qn * lax.rsqrt(_head_sum(qn * qn, ones_bd) + 1e-6) * HD ** -0.5
    k = kn * lax.rsqrt(_head_sum(kn * kn, ones_bd) + 1e-6)
    ab = pab_ref[...]
    eg = jnp.exp(-jnp.exp(alog_ref[...]) * _softplus(ab + dtb_ref[...]))
    beta = _sigmoid(ab)
    outs = []
    for h in range(NH):
        sl = slice(h * HD, (h + 1) * HD)
        egh = eg[:, h:h + 1][:, :, None]
        bh = beta[:, NH + h:NH + h + 1][:, :, None]
        S = sgdn_ref[:, h]
        kS = jnp.sum(S * _col(k[:, sl]), axis=1, keepdims=True)
        u = bh * (_rowv(v[:, sl]) - egh * kS)
        S = egh * S + _col(k[:, sl]) * u
        ngdn_ref[:, h] = S
        outs.append(jnp.sum(S * _col(q[:, sl]), axis=1))
    o = jnp.concatenate(outs, axis=1)
    ms = _head_sum(o * o, ones_bd) * (1.0 / HD)
    ygdn_ref[...] = (o * lax.rsqrt(ms + NORM_EPS) * gnw_ref[...] * _silu(P[:, CONV_CH:])).astype(BF)

    P = phg_ref[...]
    q, log_f, k = _hgrn_gates(P[:, :BR], P[:, BR:2 * BR], lb_ref[...])
    f = jnp.exp(log_f)
    v = P[:, 2 * BR:3 * BR]
    outs = []
    for h in range(NH):
        sl = slice(h * HD, (h + 1) * HD)
        S = shg_ref[:, h] * _col(f[:, sl]) + _col(k[:, sl]) * _rowv(v[:, sl])
        nhg_ref[:, h] = S
        outs.append(jnp.sum(S * _col(q[:, sl]), axis=1))
    o = jnp.concatenate(outs, axis=1)
    ms = _head_sum(o * o, ones_bd) * (1.0 / HD)
    yhg_ref[...] = (o * lax.rsqrt(ms + NORM_EPS) * hnw_ref[...] * _sigmoid(P[:, 3 * BR:])).astype(BF)


def _step_sample(ps, states, layer, lp, lb, consts, bt=8):
    prw, pret, pgdn, pab, phg = ps
    n = prw.shape[0]
    cosf, sins = _rope_tables(PAST_LEN + jnp.arange(1))
    params = [lp[nm] for nm in ("rw_mu", "rw_w0", "rw_w2", "rw_a0", "rw_a2", "rw_g2", "rw_k_k", "rw_k_a",
                                "rw_r_k", "rw_ln_w", "rw_ln_b")]
    params += [cosf, sins, lp["gdn_conv_w"], lp["gdn_a_log"], lp["gdn_dt_bias"], lp["gdn_norm_w"], lb,
               lp["hg_norm_w"], consts["ones_bd"]]
    rows = lambda w: pl.BlockSpec((bt, w), lambda i: (i, 0))
    st4 = pl.BlockSpec((bt, NH, HD, HD), lambda i: (i, 0, 0, 0))
    cvs = pl.BlockSpec((bt, GDN_CONV - 1, CONV_CH), lambda i: (i, 0, 0))
    full = lambda t: pl.BlockSpec(t.shape, lambda i: (0,) * t.ndim)
    st4_l = pl.BlockSpec((None, bt, NH, HD, HD), lambda i: (layer, i, 0, 0, 0))
    sh_l = pl.BlockSpec((None, bt, None, D_MODEL), lambda i: (layer, i, 0, 0))
    cvs_l = pl.BlockSpec((None, bt, GDN_CONV - 1, CONV_CH), lambda i: (layer, i, 0, 0))
    srw, ssh, sret, sgdn, scv, shg = states
    return pl.pallas_call(
        _step_body,
        grid=(n // bt,),
        in_specs=[rows(D_MODEL), rows(D_MODEL), rows(D_MODEL), rows(LANES), rows(D_MODEL),
                  st4_l, sh_l, st4_l, st4_l, cvs_l, st4_l] + [full(t) for t in params],
        out_specs=[rows(BR)] * 4 + [st4, rows(D_MODEL), st4, st4, cvs, st4],
        out_shape=[jax.ShapeDtypeStruct((n, BR), BF)] * 4
        + [jax.ShapeDtypeStruct(srw.shape[1:], F32), jax.ShapeDtypeStruct((n, D_MODEL), F32),
           jax.ShapeDtypeStruct(sret.shape[1:], F32), jax.ShapeDtypeStruct(sgdn.shape[1:], F32),
           jax.ShapeDtypeStruct(scv.shape[1:], F32), jax.ShapeDtypeStruct(shg.shape[1:], F32)],
        compiler_params=pltpu.CompilerParams(dimension_semantics=("arbitrary",),
                                             vmem_limit_bytes=VMEM_LIMIT),
        name="sample_step",
    )(prw, pret, pgdn, pab, phg, srw, ssh, sret, sgdn, scv, shg, *params)


def _constants():
    d = jnp.arange(BR)
    ones_bd = ((d[:, None] // HD) == (d[None, :] // HD)).astype(BF)
    consts = {"ones_bd": ones_bd, "eye": jnp.eye(HD, dtype=BF)}
    for nc in (1, 2, 4):
        i = jnp.arange(nc * CHUNK)
        tri = ((i[:, None] >= i[None, :]) & ((i[:, None] // CHUNK) == (i[None, :] // CHUNK))).astype(BF)
        consts["tri%d" % nc] = tri
        consts["triu%d" % nc] = tri.T
    consts["tri"] = consts["tri1"]
    return consts


def _layer_params(l, a):
    row = lambda t: t[l].reshape(1, -1).astype(F32)
    o_ret, o_gdn = D_MODEL, 2 * D_MODEL
    o_ab = o_gdn + 4 * BR
    o_hg = o_ab + 2 * NH
    o_gate = o_hg + 4 * BR
    w_in = a["w_in"][l]
    wab = jnp.pad(w_in[:, o_ab:o_hg], ((0, 0), (0, LANES - 2 * NH)))
    pad_row = lambda t: jnp.pad(t[l].reshape(1, -1).astype(F32), ((0, 0), (0, LANES - NH)))
    return {
        "norm_mix_w": row(a["norm_mix_w"]), "norm_ffn_w": row(a["norm_ffn_w"]),
        "w_mix": tuple(w.astype(BF) for w in (w_in[:, :o_ret], w_in[:, o_ret:o_gdn], w_in[:, o_gdn:o_ab], wab,
                                               w_in[:, o_hg:o_gate])),
        "w_gate": w_in[:, o_gate:].astype(BF), "w_branch": a["w_branch"][l].astype(BF),
        "w_out": a["w_out"][l].astype(BF), "w_gate_up": a["w_gate_up"][l].astype(BF),
        "w_down": a["w_down"][l].astype(BF),
        "rw_mu": row(a["rw_mu"]), "rw_w0": row(a["rw_w0"]), "rw_w2": a["rw_w2"][l].astype(BF),
        "rw_a0": row(a["rw_a0"]), "rw_a2": a["rw_a2"][l].astype(BF), "rw_g2": a["rw_g2"][l].astype(BF),
        "rw_k_k": row(a["rw_k_k"]), "rw_k_a": row(a["rw_k_a"]), "rw_r_k": row(a["rw_r_k"]),
        "rw_ln_w": row(a["rw_ln_w"]), "rw_ln_b": row(a["rw_ln_b"]),
        "gdn_conv_w": a["gdn_conv_w"][l].astype(F32), "gdn_a_log": pad_row(a["gdn_a_log"]),
        "gdn_dt_bias": pad_row(a["gdn_dt_bias"]),
        "gdn_norm_w": jnp.tile(a["gdn_norm_w"][l].reshape(1, HD).astype(F32), (1, NH)),
        "hg_norm_w": row(a["hg_norm_w"]),
    }


def kernel(x_prompt, x_sample, c_prompt, c_sample, state_rwkv, state_rwkv_shift, state_ret, state_gdn, state_gdn_conv, state_hgrn, ada_w, ada_b, norm_mix_w, w_in, rw_mu, rw_w0, rw_w2, rw_a0, rw_a2, rw_g2, rw_k_k, rw_k_a, rw_r_k, rw_ln_w, rw_ln_b, gdn_conv_w, gdn_a_log, gdn_dt_bias, gdn_norm_w, hg_lb_logits, hg_norm_w, w_branch, w_out, norm_ffn_w, w_gate_up, w_down, final_norm_w):
    a = dict(norm_mix_w=norm_mix_w, w_in=w_in, rw_mu=rw_mu, rw_w0=rw_w0, rw_w2=rw_w2, rw_a0=rw_a0, rw_a2=rw_a2,
             rw_g2=rw_g2, rw_k_k=rw_k_k, rw_k_a=rw_k_a, rw_r_k=rw_r_k, rw_ln_w=rw_ln_w, rw_ln_b=rw_ln_b,
             gdn_conv_w=gdn_conv_w, gdn_a_log=gdn_a_log, gdn_dt_bias=gdn_dt_bias, gdn_norm_w=gdn_norm_w,
             hg_norm_w=hg_norm_w, w_branch=w_branch, w_out=w_out, norm_ffn_w=norm_ffn_w, w_gate_up=w_gate_up,
             w_down=w_down)
    depth = ada_w.shape[0]
    Bp, Tp, _ = x_prompt.shape
    Bs = x_sample.shape[0]
    assert x_sample.shape[1] == 1 and Tp % CHUNK == 0
    tm = next((t for t in (512, 256) if Tp % t == 0), CHUNK)
    consts = _constants()
    mod = _ada_mod(jnp.concatenate([c_prompt, c_sample], axis=0), ada_w.astype(F32), ada_b.astype(F32))
    lbs = _lower_bounds(hg_lb_logits)
    fw = final_norm_w.reshape(1, D_MODEL).astype(F32)

    xp = x_prompt
    xs = x_sample.reshape(1, Bs, D_MODEL)
    new_p = [[] for _ in range(6)]
    new_s = [[] for _ in range(6)]
    for l in range(depth):
        lp = _layer_params(l, a)
        lb = lbs[l:l + 1]
        mod_p = mod[l, :Bp].reshape(Bp, 1, -1)
        mod_s = mod[l, Bp:].reshape(1, Bs, -1)
        final = l == depth - 1

        h, prw, pret, pgdn, pab, phg = _in_proj(xp, mod_p, 1, tm, lp["norm_mix_w"], lp["w_mix"])
        y_rw, s_rw, s_sh = _rw_prompt(prw, lp, consts)
        y_ret, s_ret = _ret_prompt(pret, consts)
        y_gdn, s_gdn, s_cv = _gdn_prompt(pgdn, pab, lp["gdn_conv_w"], lp["gdn_a_log"], lp["gdn_dt_bias"],
                                         lp["gdn_norm_w"], consts)
        y_hg, s_hg = _hg_prompt(phg, lb, lp["hg_norm_w"], consts)
        xp = _merge(xp, h, (y_rw, y_ret, y_gdn, y_hg), mod_p, 1, tm, lp["w_gate"], lp["w_branch"], lp["w_out"])
        xp = _ffn(xp, mod_p, 1, tm, lp["norm_ffn_w"], lp["w_gate_up"], lp["w_down"], fw, final)
        for lst, s in zip(new_p, (s_rw, s_sh, s_ret, s_gdn, s_cv, s_hg)):
            lst.append(s)

        h, prw, pret, pgdn, pab, phg = _in_proj(xs, mod_s, Bs, Bs, lp["norm_mix_w"], lp["w_mix"])
        states = (state_rwkv, state_rwkv_shift, state_ret, state_gdn, state_gdn_conv, state_hgrn)
        outs = _step_sample(tuple(t[0] for t in (prw, pret, pgdn, pab, phg)), states, l, lp, lb, consts)
        ys = tuple(t[None] for t in outs[:4])
        xs = _merge(xs, h, ys, mod_s, Bs, Bs, lp["w_gate"], lp["w_branch"], lp["w_out"])
        xs = _ffn(xs, mod_s, Bs, Bs, lp["norm_ffn_w"], lp["w_gate_up"], lp["w_down"], fw, final)
        n_rw, n_sh, n_ret, n_gdn, n_cv, n_hg = outs[4:]
        for lst, s in zip(new_s, (n_rw, n_sh.reshape(Bs, 1, -1), n_ret, n_gdn, n_cv, n_hg)):
            lst.append(s)

    y_prompt = xp
    y_sample = xs.reshape(Bs, 1, D_MODEL)
    return (y_prompt, y_sample) + tuple(jnp.stack(t) for t in new_p) + tuple(jnp.stack(t) for t in new_s)
```

```python
import functools
import math

import jax
import jax.numpy as jnp
from jax import lax
from jax.experimental import pallas as pl
from jax.experimental.pallas import tpu as pltpu

F32 = jnp.float32
BF = jnp.bfloat16

D_MODEL = 1024
DEPTH = 4
PAST_LEN = 16384
BR = 256
HD = 64
NH = 4
RW_LORA_W = 64
RW_LORA_A = 64
RW_GN_EPS = 64e-5
ROPE_BASE = 10000.0
GDN_CONV = 4
CONV_CH = 3 * BR
D_FF = 2816
NORM_EPS = 1e-6
CHUNK = 64
SUB = 16
LANES = 128
VMEM_LIMIT = 56 * 1024 * 1024

NN = ((1,), (0,))
NT = ((1,), (1,))
TN = ((0,), (0,))


def _mm(a, b, dims=NN):
    return lax.dot_general(a.astype(BF), b.astype(BF), (dims, ((), ())), preferred_element_type=F32)


def _split2(a):
    hi = a.astype(BF)
    lo = (a - hi.astype(F32)).astype(BF)
    return hi, lo


def _split3(a):
    hi = a.astype(BF)
    r = a - hi.astype(F32)
    mid = r.astype(BF)
    lo = (r - mid.astype(F32)).astype(BF)
    return hi, mid, lo


def _mm_exact_rhs(a, b, dims=NN, parts=3):
    ps = _split3(a) if parts == 3 else _split2(a)
    out = _mm(ps[0], b, dims)
    for p in ps[1:]:
        out = out + _mm(p, b, dims)
    return out


def _mm_exact_lhs(a, b, dims=NN, parts=3):
    ps = _split3(b) if parts == 3 else _split2(b)
    out = _mm(a, ps[0], dims)
    for p in ps[1:]:
        out = out + _mm(a, p, dims)
    return out


def _sigmoid(x):
    return jax.nn.sigmoid(x)


def _silu(x):
    return x * jax.nn.sigmoid(x)


def _softplus(x):
    return jnp.maximum(x, 0.0) + jnp.log1p(jnp.exp(-jnp.abs(x)))


def _head_sum(x, ones_bd):
    return _mm(x, ones_bd)


def _tri_inverse(Ls, row, col):
    c = Ls[0].shape[0]
    eye = (row == col).astype(F32)
    bd = (row // SUB) == (col // SUB)
    Ps = [jnp.where(bd, L, 0.0) for L in Ls]
    Ds = [eye - P for P in Ps]
    for _ in range(int(math.log2(SUB)) - 1):
        Ps = [_mm(P, P) for P in Ps]
        Ds = [Dm + _mm(Dm, P) for Dm, P in zip(Ds, Ps)]
    size = SUB
    while size < c:
        off = ((row // (2 * size)) == (col // (2 * size))) & ((row // size) != (col // size))
        DCs = [_mm(Dm, jnp.where(off, L, 0.0)) for Dm, L in zip(Ds, Ls)]
        Ds = [Dm - _mm(DC, Dm) for Dm, DC in zip(Ds, DCs)]
        size *= 2
    return Ds


def _mod_body(c_ref, w_ref, b_ref, o_ref):
    o_ref[0] = _mm(_silu(c_ref[...]), w_ref[0]) + b_ref[0]


def _ada_mod(c_all, ada_w, ada_b):
    n = c_all.shape[0]
    ncol = ada_w.shape[2] // D_MODEL
    return pl.pallas_call(
        _mod_body,
        grid=(DEPTH, ncol),
        in_specs=[
            pl.BlockSpec((n, D_MODEL), lambda l, j: (0, 0)),
            pl.BlockSpec((1, D_MODEL, D_MODEL), lambda l, j: (l, 0, j)),
            pl.BlockSpec((1, 1, D_MODEL), lambda l, j: (l, 0, j)),
        ],
        out_specs=pl.BlockSpec((1, n, D_MODEL), lambda l, j: (l, 0, j)),
        out_shape=jax.ShapeDtypeStruct((DEPTH, n, ncol * D_MODEL), F32),
        compiler_params=pltpu.CompilerParams(dimension_semantics=("arbitrary", "arbitrary"),
                                             vmem_limit_bytes=VMEM_LIMIT),
        name="ada_mod",
    )(c_all, ada_w, ada_b.reshape(DEPTH, 1, -1))


def _lb_body(x_ref, o_ref):
    x = x_ref[...]
    m = jnp.max(x, axis=0, keepdims=True)
    e = jnp.exp(x - m)
    p = e / jnp.sum(e, axis=0, keepdims=True)
    acc = jnp.zeros_like(p[0:1])
    rows = [acc]
    for l in range(1, DEPTH):
        acc = acc + p[l:l + 1]
        rows.append(acc)
    o_ref[...] = jnp.concatenate(rows, axis=0)


def _lower_bounds(logits):
    return pl.pallas_call(
        _lb_body,
        out_shape=jax.ShapeDtypeStruct(logits.shape, F32),
        name="hgrn_lower_bounds",
    )(logits.astype(F32))


def _mod_spec(rm, tm, col):
    if rm == 1:
        return pl.BlockSpec((1, 1, D_MODEL), lambda b, i: (b, 0, col))
    return pl.BlockSpec((1, tm, D_MODEL), lambda b, i: (b, i, col))


def _full_spec(shape):
    nd = len(shape)
    return pl.BlockSpec(shape, lambda b, i: (0,) * nd)


def _weight_spec(shape):
    nd = len(shape)
    return pl.BlockSpec(shape, lambda b, i: (0,) * nd, pipeline_mode=pl.Buffered(1))


def _row_spec(tm, width):
    return pl.BlockSpec((1, tm, width), lambda b, i: (b, i, 0))


def _modulated_norm(x, nw, sc, sh):
    ms = jnp.mean(x * x, axis=-1, keepdims=True)
    return (x * lax.rsqrt(ms + NORM_EPS) * nw) * (1.0 + sc) + sh


def _in_body(x_ref, sc_ref, sh_ref, nw_ref, wrw_ref, wret_ref, wgdn_ref, wab_ref, whg_ref,
             h_ref, prw_ref, pret_ref, pgdn_ref, pab_ref, phg_ref):
    h = _modulated_norm(x_ref[0], nw_ref[...], sc_ref[0], sh_ref[0]).astype(BF)
    h_ref[0] = h
    prw_ref[0] = _mm(h, wrw_ref[...])
    pret_ref[0] = _mm(h, wret_ref[...])
    pgdn_ref[0] = _mm(h, wgdn_ref[...])
    pab_ref[0] = _mm(h, wab_ref[...])
    phg_ref[0] = _mm(h, whg_ref[...])


def _in_proj(x, mod, rm, tm, nw, ws):
    G, R, _ = x.shape
    wrw, wret, wgdn, wab, whg = ws
    outs = [jax.ShapeDtypeStruct((G, R, D_MODEL), BF)] + [
        jax.ShapeDtypeStruct((G, R, w.shape[1]), F32) for w in ws]
    return pl.pallas_call(
        _in_body,
        grid=(G, R // tm),
        in_specs=[_row_spec(tm, D_MODEL), _mod_spec(rm, tm, 1), _mod_spec(rm, tm, 0),
                  _full_spec((1, D_MODEL))] + [_weight_spec(w.shape) for w in ws],
        out_specs=[_row_spec(tm, D_MODEL)] + [_row_spec(tm, w.shape[1]) for w in ws],
        out_shape=outs,
        compiler_params=pltpu.CompilerParams(dimension_semantics=("arbitrary", "arbitrary"),
                                             vmem_limit_bytes=VMEM_LIMIT),
        name="in_proj",
    )(x, mod, mod, nw, wrw, wret, wgdn, wab, whg)


def _merge_ffn_body(final, x_ref, h_ref, yrw_ref, yret_ref, ygdn_ref, yhg_ref, g1_ref, sc_ref, sh_ref, g2_ref,
                    nw_ref, wg_ref, wb_ref, wo_ref, wgu_ref, wd_ref, fw_ref, o_ref):
    h = h_ref[0]
    merged = None
    for n, y_ref in enumerate((yrw_ref, yret_ref, ygdn_ref, yhg_ref)):
        gate = _sigmoid(_mm(h, wg_ref[:, n * D_MODEL:(n + 1) * D_MODEL]))
        term = gate * _mm(y_ref[0], wb_ref[n])
        merged = term if merged is None else merged + term
    x = x_ref[0] + g1_ref[0] * _mm(merged, wo_ref[...])
    h2 = _modulated_norm(x, nw_ref[...], sc_ref[0], sh_ref[0]).astype(BF)
    gate = _mm(h2, wgu_ref[:, :D_FF])
    up = _mm(h2, wgu_ref[:, D_FF:])
    x = x + g2_ref[0] * _mm(_silu(gate) * up, wd_ref[...])
    if final:
        ms = jnp.mean(x * x, axis=-1, keepdims=True)
        x = x * lax.rsqrt(ms + NORM_EPS) * fw_ref[...]
    o_ref[0] = x


def _merge_ffn(x, h, ys, mod, rm, tm, nw, wg, wb, wo, wgu, wd, fw, final):
    G, R, _ = x.shape
    return pl.pallas_call(
        functools.partial(_merge_ffn_body, final),
        grid=(G, R // tm),
        in_specs=[_row_spec(tm, D_MODEL), _row_spec(tm, D_MODEL)] + [_row_spec(tm, BR)] * 4
        + [_mod_spec(rm, tm, 2), _mod_spec(rm, tm, 4), _mod_spec(rm, tm, 3), _mod_spec(rm, tm, 5),
           _full_spec((1, D_MODEL))] + [_weight_spec(w.shape) for w in (wg, wb, wo, wgu, wd)]
        + [_full_spec((1, D_MODEL))],
        out_specs=_row_spec(tm, D_MODEL),
        out_shape=jax.ShapeDtypeStruct((G, R, D_MODEL), F32),
        compiler_params=pltpu.CompilerParams(dimension_semantics=("arbitrary", "arbitrary"),
                                             vmem_limit_bytes=VMEM_LIMIT),
        name="merge_ffn",
    )(x, h, *ys, mod, mod, mod, mod, nw, wg, wb, wo, wgu, wd, fw)


def _chunk_spec(c, width):
    return pl.BlockSpec((1, c, width), lambda b, i: (b, i, 0))


def _state_spec():
    return pl.BlockSpec((1, NH, HD, HD), lambda b, i: (b, 0, 0, 0))


def _iota2(shape):
    return (lax.broadcasted_iota(jnp.int32, shape, 0), lax.broadcasted_iota(jnp.int32, shape, 1))


def _rope(t, cosf, sins):
    lane = lax.broadcasted_iota(jnp.int32, t.shape, 1)
    width = t.shape[1]
    rot = jnp.where((lane % HD) < HD // 2, pltpu.roll(t, width - HD // 2, 1), pltpu.roll(t, HD // 2, 1))
    return t * cosf + rot * sins


def _ret_body(nc, p_ref, cos_ref, sin_ref, qin_ref, kout_ref, dmat_ref, cdec_ref, ones_ref, y_ref, s_ref):
    c = CHUNK

    @pl.when(pl.program_id(1) == 0)
    def _():
        s_ref[...] = jnp.zeros_like(s_ref)

    P = p_ref[0]
    cosf, sins = cos_ref[...], sin_ref[...]
    q = _rope(P[:, :BR], cosf, sins)
    k = _rope(P[:, BR:2 * BR], cosf, sins) * HD ** -0.5
    v = P[:, 2 * BR:3 * BR]
    g = P[:, 3 * BR:]
    qs = q * qin_ref[...]
    ko = k * kout_ref[...]
    items = [(ci, h) for ci in range(nc) for h in range(NH)]

    def sl(t, ci, h):
        return t[ci * c:(ci + 1) * c, h * HD:(h + 1) * HD]

    ss = [_mm(sl(q, ci, h), sl(k, ci, h), NT) * dmat_ref[h] for ci, h in items]
    oi = [_mm(s, sl(v, ci, h)) for s, (ci, h) in zip(ss, items)]
    kv = [_mm(sl(ko, ci, h), sl(v, ci, h), TN) for ci, h in items]
    outs = [[None] * NH for _ in range(nc)]
    for idx, (ci, h) in enumerate(items):
        S = s_ref[0, h]
        outs[ci][h] = oi[idx] + _mm(sl(qs, ci, h), S)
        s_ref[0, h] = S * cdec_ref[h] + kv[idx]
    o = jnp.concatenate([jnp.concatenate(r, axis=1) for r in outs], axis=0)
    ms = _head_sum(o * o, ones_ref[...]) * (1.0 / HD)
    y_ref[0] = (o * lax.rsqrt(ms + NORM_EPS) * _silu(g)).astype(BF)


def _ret_tables(c):
    lg = jnp.log1p(-jnp.exp2(-5.0 - jnp.arange(NH, dtype=F32)))[:, None]
    idx = jnp.arange(c, dtype=F32)
    diff = idx[:, None] - idx[None, :]
    dmat = jnp.where(diff >= 0, jnp.exp(lg[:, :, None] * jnp.maximum(diff, 0.0)), 0.0)
    q_in = jnp.exp(lg * (idx + 1.0))
    k_out = jnp.exp(lg * (c - 1.0 - idx))
    c_dec = jnp.exp(lg * c)
    expand = lambda t: jnp.repeat(t.T, HD, axis=1)
    return dmat, expand(q_in), expand(k_out), jnp.broadcast_to(c_dec[:, :, None], (NH, HD, HD))


def _rope_tables(pos):
    half = HD // 2
    freqs = ROPE_BASE ** (-jnp.arange(half, dtype=F32) / half)
    ang = pos.astype(F32)[:, None] * freqs
    cos, sin = jnp.cos(ang), jnp.sin(ang)
    cosf = jnp.tile(jnp.concatenate([cos, cos], axis=1), (1, NH))
    sins = jnp.tile(jnp.concatenate([-sin, sin], axis=1), (1, NH))
    return cosf, sins


def _ret_prompt(p, consts):
    B, T, _ = p.shape
    c = CHUNK
    nc = _chunks_per_step(T)
    tb = nc * c
    cosf, sins = _rope_tables(jnp.arange(T))
    dmat, q_in, k_out, c_dec = _ret_tables(c)
    q_in, k_out = jnp.tile(q_in, (nc, 1)), jnp.tile(k_out, (nc, 1))
    return pl.pallas_call(
        functools.partial(_ret_body, nc),
        grid=(B, T // tb),
        in_specs=[_chunk_spec(tb, D_MODEL),
                  pl.BlockSpec((tb, BR), lambda b, i: (i, 0)), pl.BlockSpec((tb, BR), lambda b, i: (i, 0)),
                  _full_spec((tb, BR)), _full_spec((tb, BR)), _full_spec((NH, c, c)), _full_spec((NH, HD, HD)),
                  _full_spec((BR, BR))],
        out_specs=[_chunk_spec(tb, BR), _state_spec()],
        out_shape=[jax.ShapeDtypeStruct((B, T, BR), BF), jax.ShapeDtypeStruct((B, NH, HD, HD), F32)],
        compiler_params=pltpu.CompilerParams(dimension_semantics=("arbitrary", "arbitrary"),
                                             vmem_limit_bytes=VMEM_LIMIT),
        name="ret_prompt",
    )(p, cosf, sins, q_in, k_out, dmat, c_dec, consts["ones_bd"])


def _chunks_per_step(T):
    for nc in (8, 4, 2):
        if T % (nc * CHUNK) == 0:
            return nc
    return 1


def _gdn_body(nc, pg_ref, pab_ref, cw_ref, alog_ref, dtb_ref, nw_ref, tri_ref, triu_ref, ones_ref,
              y_ref, s_ref, cv_ref, xext):
    c = CHUNK
    tb = nc * c

    @pl.when(pl.program_id(1) == 0)
    def _():
        s_ref[...] = jnp.zeros_like(s_ref)
        xext[pl.ds(0, 8), :] = jnp.zeros((8, CONV_CH), F32)

    P = pg_ref[0]
    X = P[:, :CONV_CH]
    xext[pl.ds(8, tb), :] = X
    cw = cw_ref[...]
    conv = xext[pl.ds(8 - 3, tb), :] * cw[0:1]
    conv = conv + xext[pl.ds(8 - 2, tb), :] * cw[1:2]
    conv = conv + xext[pl.ds(8 - 1, tb), :] * cw[2:3]
    conv = conv + X * cw[3:4]
    xext[pl.ds(0, 8), :] = X[tb - 8:tb]
    cv_ref[0] = X[tb - (GDN_CONV - 1):tb]
    conv = _silu(conv)
    ones_bd = ones_ref[...]
    qn, kn, v = conv[:, :BR], conv[:, BR:2 * BR], conv[:, 2 * BR:]
    q = qn * lax.rsqrt(_head_sum(qn * qn, ones_bd) + 1e-6) * HD ** -0.5
    k = kn * lax.rsqrt(_head_sum(kn * kn, ones_bd) + 1e-6)
    gate = P[:, CONV_CH:]

    ab = pab_ref[0]
    glog = -jnp.exp(alog_ref[...]) * _softplus(ab + dtb_ref[...])
    beta = _sigmoid(ab)
    Gc = _mm_exact_lhs(tri_ref[...], glog, NN)
    Gr = _mm_exact_rhs(glog, triu_ref[...], TN)
    eG = jnp.exp(Gc)
    row, col = _iota2((c, c))
    incl = row >= col
    strict = row > col
    items = [(ci, h) for ci in range(nc) for h in range(NH)]

    def sl(t, ci, h):
        return t[ci * c:(ci + 1) * c, h * HD:(h + 1) * HD]

    KQ = [_mm(jnp.concatenate([sl(k, ci, h), sl(q, ci, h)], axis=0), sl(k, ci, h), NT) for ci, h in items]
    Ls, qks, rhss, kouts, cdecs = [], [], [], [], []
    for (ci, h), kq in zip(items, KQ):
        rows = slice(ci * c, (ci + 1) * c)
        gc = Gc[rows, h:h + 1]
        gr = Gr[h:h + 1, rows]
        bcol = beta[rows, NH + h:NH + h + 1]
        gam = jnp.where(incl, jnp.exp(jnp.where(incl, gc - gr, 0.0)), 0.0)
        Ls.append(jnp.where(strict, bcol * gam * kq[:c], 0.0))
        qks.append(kq[c:] * gam)
        rhss.append(jnp.concatenate([sl(v, ci, h) * bcol, sl(k, ci, h) * (bcol * eG[rows, h:h + 1])], axis=1))
        gend = gc[c - 1:c, :]
        kouts.append(sl(k, ci, h) * jnp.exp(gend - gc))
        cdecs.append(jnp.exp(gend))
    Ts = _tri_inverse(Ls, row, col)
    sols = [_mm(T, rhs) for T, rhs in zip(Ts, rhss)]
    QS = [_mm(qk, sol) for qk, sol in zip(qks, sols)]
    KS = [_mm(kout, sol, TN) for kout, sol in zip(kouts, sols)]
    outs = [[None] * NH for _ in range(nc)]
    for idx, (ci, h) in enumerate(items):
        rows = slice(ci * c, (ci + 1) * c)
        QW = sl(q, ci, h) * eG[rows, h:h + 1] - QS[idx][:, HD:]
        S = s_ref[0, h]
        Z = _mm(jnp.concatenate([KS[idx][:, HD:], QW], axis=0), S)
        s_ref[0, h] = S * cdecs[idx] - Z[:HD] + KS[idx][:, :HD]
        outs[ci][h] = Z[HD:] + QS[idx][:, :HD]
    o = jnp.concatenate([jnp.concatenate(r, axis=1) for r in outs], axis=0)
    ms = _head_sum(o * o, ones_bd) * (1.0 / HD)
    y_ref[0] = (o * lax.rsqrt(ms + NORM_EPS) * nw_ref[...] * _silu(gate)).astype(BF)


def _gdn_prompt(pg, pab, cw, alog, dtb, nw, consts):
    B, T, _ = pg.shape
    nc = _chunks_per_step(T)
    tb = nc * CHUNK
    return pl.pallas_call(
        functools.partial(_gdn_body, nc),
        grid=(B, T // tb),
        in_specs=[_chunk_spec(tb, D_MODEL), _chunk_spec(tb, LANES), _full_spec((GDN_CONV, CONV_CH)),
                  _full_spec((1, LANES)), _full_spec((1, LANES)), _full_spec((1, BR)),
                  _full_spec((tb, tb)), _full_spec((tb, tb)), _full_spec((BR, BR))],
        out_specs=[_chunk_spec(tb, BR), _state_spec(),
                   pl.BlockSpec((1, GDN_CONV - 1, CONV_CH), lambda b, i: (b, 0, 0))],
        out_shape=[jax.ShapeDtypeStruct((B, T, BR), BF), jax.ShapeDtypeStruct((B, NH, HD, HD), F32),
                   jax.ShapeDtypeStruct((B, GDN_CONV - 1, CONV_CH), F32)],
        scratch_shapes=[pltpu.VMEM((tb + 8, CONV_CH), F32)],
        compiler_params=pltpu.CompilerParams(dimension_semantics=("arbitrary", "arbitrary"),
                                             vmem_limit_bytes=VMEM_LIMIT),
        name="gdn_prompt",
    )(pg, pab, cw, alog, dtb, nw, consts["tri%d" % nc], consts["triu%d" % nc], consts["ones_bd"])


def _hgrn_gates(xq, xf, lb):
    q = _silu(xq)
    ls = jnp.minimum(xf, 0.0) - jnp.log1p(jnp.exp(-jnp.abs(xf)))
    pos = lb > 0.0
    a = jnp.log(jnp.where(pos, lb, 1.0))
    b = jnp.log1p(-lb) + ls
    lae = jnp.maximum(a, b) + jnp.log1p(jnp.exp(-jnp.abs(a - b)))
    log_f = jnp.where(pos, lae, ls)
    k = (1.0 - lb) * _sigmoid(-xf)
    return q, log_f, k


def _hg_body(nc, p_ref, lb_ref, nw_ref, tri_ref, ones_ref, eye_ref, y_ref, s_ref, st, tbuf, sbuf):
    c = CHUNK
    tb = nc * c
    nb = tb // SUB
    half = SUB // 2
    gi = pl.program_id(1)

    @pl.when(gi == 0)
    def _():
        st[...] = jnp.zeros_like(st)

    P = p_ref[0]
    q, log_f, k = _hgrn_gates(P[:, :BR], P[:, BR:2 * BR], lb_ref[...])
    v = P[:, 2 * BR:3 * BR]
    og = P[:, 3 * BR:]
    ones_bd = ones_ref[...]
    G = _mm_exact_lhs(tri_ref[...], log_f, NN)

    G3, q3, k3, v3 = (t.reshape(nb, SUB, BR) for t in (G, q, k, v))
    o_lo = jnp.zeros((nb, half, BR), F32)
    o_hi = jnp.zeros((nb, half, BR), F32)
    offs, total = [], 0
    for jj in range(SUB):
        offs.append(total)
        total += nb * (SUB if jj < half else half)
    for jj in range(SUB):
        i0 = 0 if jj < half else half
        rowi = lax.broadcasted_iota(jnp.int32, (nb, SUB - i0, BR), 1) + i0
        e = jnp.exp(jnp.minimum(G3[:, i0:, :] - G3[:, jj:jj + 1, :], 0.0))
        t = jnp.where(rowi >= jj, q3[:, i0:, :] * e * k3[:, jj:jj + 1, :], 0.0)
        tbuf[pl.ds(offs[jj], nb * (SUB - i0)), :] = t.reshape(nb * (SUB - i0), BR).astype(BF)
    sbuf[...] = jnp.dot(tbuf[...], ones_bd, preferred_element_type=F32)
    for jj in range(SUB):
        i0 = 0 if jj < half else half
        s = sbuf[pl.ds(offs[jj], nb * (SUB - i0)), :].reshape(nb, SUB - i0, BR)
        contrib = s * v3[:, jj:jj + 1, :]
        if i0 == 0:
            o_lo = o_lo + contrib[:, :half]
            o_hi = o_hi + contrib[:, half:]
        else:
            o_hi = o_hi + contrib
    o = jnp.concatenate([o_lo, o_hi], axis=1).reshape(tb, BR)

    pairs = [(ci, bi) for ci in range(nc) for bi in range(1, c // SUB)]
    qts, kts = [], []
    for ci, bi in pairs:
        r0 = ci * c
        ref = G[r0 + bi * SUB - 1:r0 + bi * SUB, :]
        rows_i = slice(r0 + bi * SUB, r0 + (bi + 1) * SUB)
        rows_j = slice(r0, r0 + bi * SUB)
        qts.append(q[rows_i] * jnp.exp(jnp.minimum(G[rows_i] - ref, 0.0)))
        kts.append(k[rows_j] * jnp.exp(jnp.minimum(ref - G[rows_j], 0.0)))
    wts = [[_mm(qt[:, h * HD:(h + 1) * HD], kt[:, h * HD:(h + 1) * HD], NT) for h in range(NH)]
           for qt, kt in zip(qts, kts)]
    offs = [[_mm(w[h], v[ci * c:ci * c + bi * SUB, h * HD:(h + 1) * HD]) for h in range(NH)]
            for (ci, bi), w in zip(pairs, wts)]
    off_rows = []
    idx = 0
    for ci in range(nc):
        off_rows.append(jnp.zeros((SUB, BR), F32))
        for bi in range(1, c // SUB):
            off_rows.append(jnp.concatenate(offs[idx], axis=1))
            idx += 1
    o = o + jnp.concatenate(off_rows, axis=0)

    qg = q * jnp.exp(G)
    GE = jnp.concatenate([jnp.broadcast_to(G[(ci + 1) * c - 1:(ci + 1) * c], (c, BR)) for ci in range(nc)], axis=0)
    kd = k * jnp.exp(GE - G)
    items = [(ci, h) for ci in range(nc) for h in range(NH)]

    def sl(t, ci, h):
        return t[ci * c:(ci + 1) * c, h * HD:(h + 1) * HD]

    vk = [_mm(sl(v, ci, h), sl(kd, ci, h), TN) for ci, h in items]
    outs = [[None] * NH for _ in range(nc)]
    for idx, (ci, h) in enumerate(items):
        ST = st[h]
        outs[ci][h] = _mm(sl(qg, ci, h), ST, NT)
        st[h] = ST * jnp.exp(G[(ci + 1) * c - 1:(ci + 1) * c, h * HD:(h + 1) * HD]) + vk[idx]
    o = o + jnp.concatenate([jnp.concatenate(r, axis=1) for r in outs], axis=0)
    ms = _head_sum(o * o, ones_bd) * (1.0 / HD)
    y_ref[0] = (o * lax.rsqrt(ms + NORM_EPS) * nw_ref[...] * _sigmoid(og)).astype(BF)

    @pl.when(gi == pl.num_programs(1) - 1)
    def _():
        for h in range(NH):
            s_ref[0, h] = _mm_exact_lhs(eye_ref[...], st[h], NT)


def _hg_prompt(p, lb, nw, consts):
    B, T, _ = p.shape
    nc = _chunks_per_step(T)
    tb = nc * CHUNK
    return pl.pallas_call(
        functools.partial(_hg_body, nc),
        grid=(B, T // tb),
        in_specs=[_chunk_spec(tb, D_MODEL), _full_spec((1, BR)), _full_spec((1, BR)), _full_spec((tb, tb)),
                  _full_spec((BR, BR)), _full_spec((HD, HD))],
        out_specs=[_chunk_spec(tb, BR), _state_spec()],
        out_shape=[jax.ShapeDtypeStruct((B, T, BR), BF), jax.ShapeDtypeStruct((B, NH, HD, HD), F32)],
        scratch_shapes=[pltpu.VMEM((NH, HD, HD), F32),
                        pltpu.VMEM((tb // SUB * (SUB + SUB // 2) * (SUB // 2), BR), BF),
                        pltpu.VMEM((tb // SUB * (SUB + SUB // 2) * (SUB // 2), BR), F32)],
        compiler_params=pltpu.CompilerParams(dimension_semantics=("arbitrary", "arbitrary"),
                                             vmem_limit_bytes=VMEM_LIMIT),
        name="hgrn_prompt",
    )(p, lb, nw, consts["tri%d" % nc], consts["ones_bd"], consts["eye"])


def _rw_features(P, prev, mu, w0, w2, a0, a2, g2, kkw, kaw, ones_bd):
    Pm = P + mu * (prev - P)
    r, kx, vx = Pm[:, :BR], Pm[:, BR:2 * BR], Pm[:, 2 * BR:3 * BR]
    o4 = 3 * BR + RW_LORA_W
    o5 = o4 + RW_LORA_A
    wl, al, gl = Pm[:, 3 * BR:o4], Pm[:, o4:o5], Pm[:, o5:]
    zw = w0 + _mm(jnp.tanh(wl), w2)
    lw = -jnp.exp(-_softplus(-zw) - 0.5)
    a = _sigmoid(a0 + _mm(al, a2))
    g = _mm(_sigmoid(gl), g2)
    kks = kx * kkw
    kk = kks * lax.rsqrt(_head_sum(kks * kks, ones_bd) + 1e-6)
    k = kx * (1.0 + (a - 1.0) * kaw)
    return r, lw, k, vx, -kk, kk * a, g


def _rw_output(o, r, k, vx, g, rkw, lnw, lnb, ones_bd):
    mean = _head_sum(o, ones_bd) * (1.0 / HD)
    cen = o - mean
    var = _head_sum(cen * cen, ones_bd) * (1.0 / HD)
    on = cen * lax.rsqrt(var + RW_GN_EPS) * lnw + lnb
    bonus = _head_sum(r * k * rkw, ones_bd) * vx
    return (on + bonus) * g


def _rw_body(nc, p_ref, mu_ref, w0_ref, w2_ref, a0_ref, a2_ref, g2_ref, kkw_ref, kaw_ref, rkw_ref,
             lnw_ref, lnb_ref, tri_ref, ones_ref, eye_ref, y_ref, s_ref, sh_ref, st, prev):
    c = CHUNK
    tb = nc * c
    gi = pl.program_id(1)

    @pl.when(gi == 0)
    def _():
        st[...] = jnp.zeros_like(st)
        prev[...] = jnp.zeros_like(prev)

    P = p_ref[0]
    rowp = lax.broadcasted_iota(jnp.int32, P.shape, 0)
    shifted = jnp.where(rowp == 0, prev[0:1, :], pltpu.roll(P, 1, 0))
    prev[0:1, :] = P[tb - 1:tb, :]
    sh_ref[0] = P[tb - 1:tb, :]
    ones_bd = ones_ref[...]
    r, lw, k, vx, av, bv, g = _rw_features(P, shifted, mu_ref[...], w0_ref[...], w2_ref[...], a0_ref[...],
                                           a2_ref[...], g2_ref[...], kkw_ref[...], kaw_ref[...], ones_bd)
    LG = _mm_exact_lhs(tri_ref[...], lw, NN)
    LGE = jnp.concatenate([jnp.broadcast_to(LG[(ci + 1) * c - 1:(ci + 1) * c], (c, BR)) for ci in range(nc)],
                          axis=0)
    rt = r * jnp.exp(LG)
    at = av * jnp.exp(LG - lw)
    einv = jnp.exp(-LG)
    kh_ = k * einv
    bh_ = bv * einv
    eo = jnp.exp(LGE - LG)
    kout = k * eo
    bout = bv * eo
    row, col = _iota2((c, c))
    incl = row >= col
    strict = row > col
    items = [(ci, h) for ci in range(nc) for h in range(NH)]

    def sl(t, ci, h):
        return t[ci * c:(ci + 1) * c, h * HD:(h + 1) * HD]

    Ms = [_mm(jnp.concatenate([sl(at, ci, h), sl(rt, ci, h)], axis=0),
              jnp.concatenate([sl(bh_, ci, h), sl(kh_, ci, h)], axis=0), NT) for ci, h in items]
    Aabs = [jnp.where(strict, M[:c, :c], 0.0) for M in Ms]
    Arbs = [jnp.where(incl, M[c:, :c], 0.0) for M in Ms]
    AKs = [jnp.concatenate([jnp.where(strict, M[:c, c:], 0.0), jnp.where(incl, M[c:, c:], 0.0)], axis=0)
           for M in Ms]
    AVs = [_mm(AK, sl(vx, ci, h)) for AK, (ci, h) in zip(AKs, items)]
    Ts = _tri_inverse([-A for A in Aabs], row, col)
    WTs = [_mm(T, jnp.concatenate([AV[:c], sl(at, ci, h)], axis=1))
           for T, AV, (ci, h) in zip(Ts, AVs, items)]
    AWs = [_mm(Arb, WT) for Arb, WT in zip(Arbs, WTs)]
    CBs = [_mm(WT, sl(bout, ci, h), TN) for WT, (ci, h) in zip(WTs, items)]
    VKs = [_mm(sl(vx, ci, h), sl(kout, ci, h), TN) for ci, h in items]
    outs = [[None] * NH for _ in range(nc)]
    for idx, (ci, h) in enumerate(items):
        dec = jnp.exp(LG[(ci + 1) * c - 1:(ci + 1) * c, h * HD:(h + 1) * HD])
        RQ = sl(rt, ci, h) + AWs[idx][:, HD:]
        ST = st[h]
        outs[ci][h] = _mm(RQ, ST, NT) + AWs[idx][:, :HD] + AVs[idx][c:]
        st[h] = ST * dec + _mm(ST, CBs[idx][HD:]) + CBs[idx][:HD] + VKs[idx]
    o = jnp.concatenate([jnp.concatenate(rw, axis=1) for rw in outs], axis=0)
    y_ref[0] = _rw_output(o, r, k, vx, g, rkw_ref[...], lnw_ref[...], lnb_ref[...], ones_bd).astype(BF)

    @pl.when(gi == pl.num_programs(1) - 1)
    def _():
        for h in range(NH):
            s_ref[0, h] = _mm_exact_lhs(eye_ref[...], st[h], NT)


def _rw_prompt(p, lp, consts):
    B, T, _ = p.shape
    nc = _chunks_per_step(T)
    tb = nc * CHUNK
    params = [lp[n] for n in ("rw_mu", "rw_w0", "rw_w2", "rw_a0", "rw_a2", "rw_g2", "rw_k_k", "rw_k_a",
                              "rw_r_k", "rw_ln_w", "rw_ln_b")]
    return pl.pallas_call(
        functools.partial(_rw_body, nc),
        grid=(B, T // tb),
        in_specs=[_chunk_spec(tb, D_MODEL)] + [_full_spec(t.shape) for t in params]
        + [_full_spec((tb, tb)), _full_spec((BR, BR)), _full_spec((HD, HD))],
        out_specs=[_chunk_spec(tb, BR), _state_spec(),
                   pl.BlockSpec((1, 1, D_MODEL), lambda b, i: (b, 0, 0))],
        out_shape=[jax.ShapeDtypeStruct((B, T, BR), BF), jax.ShapeDtypeStruct((B, NH, HD, HD), F32),
                   jax.ShapeDtypeStruct((B, 1, D_MODEL), F32)],
        scratch_shapes=[pltpu.VMEM((NH, HD, HD), F32), pltpu.VMEM((8, D_MODEL), F32)],
        compiler_params=pltpu.CompilerParams(dimension_semantics=("arbitrary", "arbitrary"),
                                             vmem_limit_bytes=VMEM_LIMIT),
        name="rwkv_prompt",
    )(p, *params, consts["tri%d" % nc], consts["ones_bd"], consts["eye"])


_VT = ("rw_a", "rw_w", "rw_b", "rw_k", "rw_r", "rw_v", "ret_q", "ret_k", "ret_v", "gdn_q", "gdn_k", "gdn_v",
       "hg_f", "hg_k", "hg_q", "hg_v")
_VT_EG, _VT_BETA = len(_VT), len(_VT) + 1


def _stepT_body(prw_ref, pret_ref, pgdn_ref, pab_ref, phg_ref,
                srw_ref, ssh_ref, sret_ref, sgdn_ref, scv_ref, shg_ref,
                mu_ref, w0_ref, w2_ref, a0_ref, a2_ref, g2_ref, kkw_ref, kaw_ref, rkw_ref, lnw_ref, lnb_ref,
                cos_ref, sin_ref, gam_ref, cw_ref, alog_ref, dtb_ref, gnw_ref, lb_ref, hnw_ref, ones_ref,
                yrw_ref, yret_ref, ygdn_ref, yhg_ref,
                nrw_ref, nsh_ref, nret_ref, ngdn_ref, ncv_ref, nhg_ref, vt, ot):
    h = pl.program_id(0)
    ones_bd = ones_ref[...]

    def features():
        f = {}
        r, lw, k, vx, av, bv, g = _rw_features(prw_ref[...], ssh_ref[...], mu_ref[...], w0_ref[...], w2_ref[...],
                                               a0_ref[...], a2_ref[...], g2_ref[...], kkw_ref[...], kaw_ref[...],
                                               ones_bd)
        f.update(rw_a=av, rw_w=jnp.exp(lw), rw_b=bv, rw_k=k, rw_r=r, rw_v=vx, rw_g=g)
        P = pret_ref[...]
        cosf, sins = cos_ref[...], sin_ref[...]
        f.update(ret_q=_rope(P[:, :BR], cosf, sins), ret_k=_rope(P[:, BR:2 * BR], cosf, sins) * HD ** -0.5,
                 ret_v=P[:, 2 * BR:3 * BR])
        P = pgdn_ref[...]
        cw = cw_ref[...]
        conv = (scv_ref[:, 0, :] * cw[0:1] + scv_ref[:, 1, :] * cw[1:2] + scv_ref[:, 2, :] * cw[2:3]
                + P[:, :CONV_CH] * cw[3:4])
        conv = _silu(conv)
        qn, kn = conv[:, :BR], conv[:, BR:2 * BR]
        ab = pab_ref[...]
        f.update(gdn_q=qn * lax.rsqrt(_head_sum(qn * qn, ones_bd) + 1e-6) * HD ** -0.5,
                 gdn_k=kn * lax.rsqrt(_head_sum(kn * kn, ones_bd) + 1e-6), gdn_v=conv[:, 2 * BR:],
                 eg=jnp.exp(-jnp.exp(alog_ref[...]) * _softplus(ab + dtb_ref[...])), beta=_sigmoid(ab))
        P = phg_ref[...]
        q, log_f, k = _hgrn_gates(P[:, :BR], P[:, BR:2 * BR], lb_ref[...])
        f.update(hg_f=jnp.exp(log_f), hg_k=k, hg_q=q, hg_v=P[:, 2 * BR:3 * BR])
        return f

    @pl.when(h == 0)
    def _():
        f = features()
        nsh_ref[...] = prw_ref[...]
        ncv_ref[:, 0, :] = scv_ref[:, 1, :]
        ncv_ref[:, 1, :] = scv_ref[:, 2, :]
        ncv_ref[:, 2, :] = pgdn_ref[:, :CONV_CH]
        for i, name in enumerate(_VT):
            vt[i] = f[name].T
        vt[_VT_EG, 0:LANES] = f["eg"].T
        vt[_VT_BETA, 0:LANES] = f["beta"].T

    r0 = pl.multiple_of(h * HD, HD)

    def vec(name):
        return vt[_VT.index(name), pl.ds(r0, HD), :]

    def kcol(name):
        return vec(name)[:, None, :]

    def vrow(name):
        return vec(name)[None, :, :]

    S = srw_ref[...]
    sa = jnp.sum(kcol("rw_a") * S, axis=0)
    S = kcol("rw_w") * S + kcol("rw_b") * sa[None] + kcol("rw_k") * vrow("rw_v")
    nrw_ref[...] = S
    ot[0, pl.ds(r0, HD), :] = jnp.sum(kcol("rw_r") * S, axis=0)

    S = sret_ref[...] * gam_ref[pl.ds(h, 1), :] + kcol("ret_k") * vrow("ret_v")
    nret_ref[...] = S
    ot[1, pl.ds(r0, HD), :] = jnp.sum(kcol("ret_q") * S, axis=0)

    eg = vt[_VT_EG, pl.ds(h, 1), :]
    bh = vt[_VT_BETA, pl.ds(NH + h, 1), :]
    S = sgdn_ref[...]
    kS = jnp.sum(kcol("gdn_k") * S, axis=0)
    u = bh * (vec("gdn_v") - eg * kS)
    S = eg * S + kcol("gdn_k") * u[None]
    ngdn_ref[...] = S
    ot[2, pl.ds(r0, HD), :] = jnp.sum(kcol("gdn_q") * S, axis=0)

    S = shg_ref[...] * kcol("hg_f") + kcol("hg_k") * vrow("hg_v")
    nhg_ref[...] = S
    ot[3, pl.ds(r0, HD), :] = jnp.sum(kcol("hg_q") * S, axis=0)

    @pl.when(h == NH - 1)
    def _():
        f = features()
        yrw_ref[...] = _rw_output(ot[0].T, f["rw_r"], f["rw_k"], f["rw_v"], f["rw_g"], rkw_ref[...], lnw_ref[...],
                                  lnb_ref[...], ones_bd).astype(BF)
        o = ot[1].T
        ms = _head_sum(o * o, ones_bd) * (1.0 / HD)
        yret_ref[...] = (o * lax.rsqrt(ms + NORM_EPS) * _silu(pret_ref[:, 3 * BR:])).astype(BF)
        o = ot[2].T
        ms = _head_sum(o * o, ones_bd) * (1.0 / HD)
        ygdn_ref[...] = (o * lax.rsqrt(ms + NORM_EPS) * gnw_ref[...] * _silu(pgdn_ref[:, CONV_CH:])).astype(BF)
        o = ot[3].T
        ms = _head_sum(o * o, ones_bd) * (1.0 / HD)
        yhg_ref[...] = (o * lax.rsqrt(ms + NORM_EPS) * hnw_ref[...] * _sigmoid(phg_ref[:, 3 * BR:])).astype(BF)


def _stepT_sample(ps, states_t, shift, conv, layer, lp, lb, consts):
    prw, pret, pgdn, pab, phg = ps
    n = prw.shape[0]
    cosf, sins = _rope_tables(PAST_LEN + jnp.arange(1))
    gam = jnp.broadcast_to((1.0 - jnp.exp2(-5.0 - jnp.arange(8, dtype=F32)))[:, None], (8, n))
    params = [lp[nm] for nm in ("rw_mu", "rw_w0", "rw_w2", "rw_a0", "rw_a2", "rw_g2", "rw_k_k", "rw_k_a",
                                "rw_r_k", "rw_ln_w", "rw_ln_b")]
    params += [cosf, sins, gam, lp["gdn_conv_w"], lp["gdn_a_log"], lp["gdn_dt_bias"], lp["gdn_norm_w"], lb,
               lp["hg_norm_w"], consts["ones_bd"]]
    full = lambda shape: pl.BlockSpec(shape, lambda h: (0,) * len(shape))
    st_in = pl.BlockSpec((None, None, HD, HD, n), lambda h: (layer, h, 0, 0, 0))
    st_out = pl.BlockSpec((None, HD, HD, n), lambda h: (h, 0, 0, 0))
    sh_in = pl.BlockSpec((None, n, None, D_MODEL), lambda h: (layer, 0, 0, 0))
    cv_in = pl.BlockSpec((None, n, GDN_CONV - 1, CONV_CH), lambda h: (layer, 0, 0, 0))
    st_shape = jax.ShapeDtypeStruct((NH, HD, HD, n), F32)
    return pl.pallas_call(
        _stepT_body,
        grid=(NH,),
        in_specs=[full((n, D_MODEL)), full((n, D_MODEL)), full((n, D_MODEL)), full((n, LANES)), full((n, D_MODEL)),
                  st_in, sh_in, st_in, st_in, cv_in, st_in] + [full(t.shape) for t in params],
        out_specs=[full((n, BR))] * 4 + [st_out, full((n, D_MODEL)), st_out, st_out,
                                         full((n, GDN_CONV - 1, CONV_CH)), st_out],
        out_shape=[jax.ShapeDtypeStruct((n, BR), BF)] * 4
        + [st_shape, jax.ShapeDtypeStruct((n, D_MODEL), F32), st_shape, st_shape,
           jax.ShapeDtypeStruct((n, GDN_CONV - 1, CONV_CH), F32), st_shape],
        scratch_shapes=[pltpu.VMEM((len(_VT) + 2, BR, n), F32), pltpu.VMEM((4, BR, n), F32)],
        compiler_params=pltpu.CompilerParams(dimension_semantics=("arbitrary",),
                                             vmem_limit_bytes=VMEM_LIMIT),
        name="sample_step",
    )(prw, pret, pgdn, pab, phg, states_t[0], shift, states_t[1], states_t[2], conv, states_t[3], *params)


def _constants():
    d = jnp.arange(BR)
    ones_bd = ((d[:, None] // HD) == (d[None, :] // HD)).astype(BF)
    consts = {"ones_bd": ones_bd, "eye": jnp.eye(HD, dtype=BF)}
    for nc in (1, 2, 4, 8):
        i = jnp.arange(nc * CHUNK)
        tri = ((i[:, None] >= i[None, :]) & ((i[:, None] // CHUNK) == (i[None, :] // CHUNK))).astype(BF)
        consts["tri%d" % nc] = tri
        consts["triu%d" % nc] = tri.T
    consts["tri"] = consts["tri1"]
    return consts


def _layer_params(l, a):
    row = lambda t: t[l].reshape(1, -1).astype(F32)
    o_ret, o_gdn = D_MODEL, 2 * D_MODEL
    o_ab = o_gdn + 4 * BR
    o_hg = o_ab + 2 * NH
    o_gate = o_hg + 4 * BR
    w_in = a["w_in"][l]
    wab = jnp.pad(w_in[:, o_ab:o_hg], ((0, 0), (0, LANES - 2 * NH)))
    pad_row = lambda t: jnp.pad(t[l].reshape(1, -1).astype(F32), ((0, 0), (0, LANES - NH)))
    return {
        "norm_mix_w": row(a["norm_mix_w"]), "norm_ffn_w": row(a["norm_ffn_w"]),
        "w_mix": tuple(w.astype(BF) for w in (w_in[:, :o_ret], w_in[:, o_ret:o_gdn], w_in[:, o_gdn:o_ab], wab,
                                               w_in[:, o_hg:o_gate])),
        "w_gate": w_in[:, o_gate:].astype(BF), "w_branch": a["w_branch"][l].astype(BF),
        "w_out": a["w_out"][l].astype(BF), "w_gate_up": a["w_gate_up"][l].astype(BF),
        "w_down": a["w_down"][l].astype(BF),
        "rw_mu": row(a["rw_mu"]), "rw_w0": row(a["rw_w0"]), "rw_w2": a["rw_w2"][l].astype(BF),
        "rw_a0": row(a["rw_a0"]), "rw_a2": a["rw_a2"][l].astype(BF), "rw_g2": a["rw_g2"][l].astype(BF),
        "rw_k_k": row(a["rw_k_k"]), "rw_k_a": row(a["rw_k_a"]), "rw_r_k": row(a["rw_r_k"]),
        "rw_ln_w": row(a["rw_ln_w"]), "rw_ln_b": row(a["rw_ln_b"]),
        "gdn_conv_w": a["gdn_conv_w"][l].astype(F32), "gdn_a_log": pad_row(a["gdn_a_log"]),
        "gdn_dt_bias": pad_row(a["gdn_dt_bias"]),
        "gdn_norm_w": jnp.tile(a["gdn_norm_w"][l].reshape(1, HD).astype(F32), (1, NH)),
        "hg_norm_w": row(a["hg_norm_w"]),
    }


def kernel(x_prompt, x_sample, c_prompt, c_sample, state_rwkv, state_rwkv_shift, state_ret, state_gdn, state_gdn_conv, state_hgrn, ada_w, ada_b, norm_mix_w, w_in, rw_mu, rw_w0, rw_w2, rw_a0, rw_a2, rw_g2, rw_k_k, rw_k_a, rw_r_k, rw_ln_w, rw_ln_b, gdn_conv_w, gdn_a_log, gdn_dt_bias, gdn_norm_w, hg_lb_logits, hg_norm_w, w_branch, w_out, norm_ffn_w, w_gate_up, w_down, final_norm_w):
    a = dict(norm_mix_w=norm_mix_w, w_in=w_in, rw_mu=rw_mu, rw_w0=rw_w0, rw_w2=rw_w2, rw_a0=rw_a0, rw_a2=rw_a2,
             rw_g2=rw_g2, rw_k_k=rw_k_k, rw_k_a=rw_k_a, rw_r_k=rw_r_k, rw_ln_w=rw_ln_w, rw_ln_b=rw_ln_b,
             gdn_conv_w=gdn_conv_w, gdn_a_log=gdn_a_log, gdn_dt_bias=gdn_dt_bias, gdn_norm_w=gdn_norm_w,
             hg_norm_w=hg_norm_w, w_branch=w_branch, w_out=w_out, norm_ffn_w=norm_ffn_w, w_gate_up=w_gate_up,
             w_down=w_down)
    depth = ada_w.shape[0]
    Bp, Tp, _ = x_prompt.shape
    Bs = x_sample.shape[0]
    assert x_sample.shape[1] == 1 and Tp % CHUNK == 0
    tm = next((t for t in (512, 256) if Tp % t == 0), CHUNK)
    tm2 = min(tm, 256)
    consts = _constants()
    mod = _ada_mod(jnp.concatenate([c_prompt, c_sample], axis=0), ada_w.astype(F32), ada_b.astype(F32))
    lbs = _lower_bounds(hg_lb_logits)
    fw = final_norm_w.reshape(1, D_MODEL).astype(F32)

    xp = x_prompt
    xs = x_sample.reshape(1, Bs, D_MODEL)
    states_t = tuple(jnp.transpose(s, (0, 2, 3, 4, 1)) for s in (state_rwkv, state_ret, state_gdn, state_hgrn))
    new_p = [[] for _ in range(6)]
    new_s = [[] for _ in range(6)]
    for l in range(depth):
        lp = _layer_params(l, a)
        lb = lbs[l:l + 1]
        mod_p = mod[l, :Bp].reshape(Bp, 1, -1)
        mod_s = mod[l, Bp:].reshape(1, Bs, -1)
        final = l == depth - 1

        h, prw, pret, pgdn, pab, phg = _in_proj(xp, mod_p, 1, tm, lp["norm_mix_w"], lp["w_mix"])
        y_rw, s_rw, s_sh = _rw_prompt(prw, lp, consts)
        y_ret, s_ret = _ret_prompt(pret, consts)
        y_gdn, s_gdn, s_cv = _gdn_prompt(pgdn, pab, lp["gdn_conv_w"], lp["gdn_a_log"], lp["gdn_dt_bias"],
                                         lp["gdn_norm_w"], consts)
        y_hg, s_hg = _hg_prompt(phg, lb, lp["hg_norm_w"], consts)
        xp = _merge_ffn(xp, h, (y_rw, y_ret, y_gdn, y_hg), mod_p, 1, tm2, lp["norm_ffn_w"], lp["w_gate"],
                        lp["w_branch"], lp["w_out"], lp["w_gate_up"], lp["w_down"], fw, final)
        for lst, s in zip(new_p, (s_rw, s_sh, s_ret, s_gdn, s_cv, s_hg)):
            lst.append(s)

        h, prw, pret, pgdn, pab, phg = _in_proj(xs, mod_s, Bs, Bs, lp["norm_mix_w"], lp["w_mix"])
        outs = _stepT_sample(tuple(t[0] for t in (prw, pret, pgdn, pab, phg)), states_t, state_rwkv_shift,
                             state_gdn_conv, l, lp, lb, consts)
        ys = tuple(t[None] for t in outs[:4])
        xs = _merge_ffn(xs, h, ys, mod_s, Bs, Bs, lp["norm_ffn_w"], lp["w_gate"], lp["w_branch"], lp["w_out"],
                        lp["w_gate_up"], lp["w_down"], fw, final)
        n_rw, n_sh, n_ret, n_gdn, n_cv, n_hg = outs[4:]
        for lst, s in zip(new_s, (n_rw, n_sh.reshape(Bs, 1, -1), n_ret, n_gdn, n_cv, n_hg)):
            lst.append(s)

    y_prompt = xp
    y_sample = xs.reshape(Bs, 1, D_MODEL)
    s_out = [jnp.stack(t) for t in new_s]
    for i in (0, 2, 3, 5):
        s_out[i] = jnp.transpose(s_out[i], (0, 4, 1, 2, 3))
    return (y_prompt, y_sample) + tuple(jnp.stack(t) for t in new_p) + tuple(s_out)
```

```python
import functools
import math

import jax
import jax.numpy as jnp
from jax import lax
from jax.experimental import pallas as pl
from jax.experimental.pallas import tpu as pltpu

F32 = jnp.float32
BF = jnp.bfloat16

D_MODEL = 1024
DEPTH = 4
PAST_LEN = 16384
BR = 256
HD = 64
NH = 4
RW_LORA_W = 64
RW_LORA_A = 64
RW_GN_EPS = 64e-5
ROPE_BASE = 10000.0
GDN_CONV = 4
CONV_CH = 3 * BR
D_FF = 2816
NORM_EPS = 1e-6
CHUNK = 64
SUB = 16
LANES = 128
VMEM_LIMIT = 56 * 1024 * 1024

NN = ((1,), (0,))
NT = ((1,), (1,))
TN = ((0,), (0,))


def _mm(a, b, dims=NN):
    return lax.dot_general(a.astype(BF), b.astype(BF), (dims, ((), ())), preferred_element_type=F32)


def _split2(a):
    hi = a.astype(BF)
    lo = (a - hi.astype(F32)).astype(BF)
    return hi, lo


def _split3(a):
    hi = a.astype(BF)
    r = a - hi.astype(F32)
    mid = r.astype(BF)
    lo = (r - mid.astype(F32)).astype(BF)
    return hi, mid, lo


def _mm_exact_rhs(a, b, dims=NN, parts=3):
    ps = _split3(a) if parts == 3 else _split2(a)
    out = _mm(ps[0], b, dims)
    for p in ps[1:]:
        out = out + _mm(p, b, dims)
    return out


def _mm_exact_lhs(a, b, dims=NN, parts=3):
    ps = _split3(b) if parts == 3 else _split2(b)
    out = _mm(a, ps[0], dims)
    for p in ps[1:]:
        out = out + _mm(a, p, dims)
    return out


def _sigmoid(x):
    return jax.nn.sigmoid(x)


def _silu(x):
    return x * jax.nn.sigmoid(x)


def _softplus(x):
    return jnp.maximum(x, 0.0) + jnp.log1p(jnp.exp(-jnp.abs(x)))


def _head_sum(x, ones_bd):
    return _mm(x, ones_bd)


def _tri_inverse(Ls, row, col):
    c = Ls[0].shape[0]
    eye = (row == col).astype(F32)
    bd = (row // SUB) == (col // SUB)
    Ps = [jnp.where(bd, L, 0.0) for L in Ls]
    Ds = [eye - P for P in Ps]
    for _ in range(int(math.log2(SUB)) - 1):
        Ps = [_mm(P, P) for P in Ps]
        Ds = [Dm + _mm(Dm, P) for Dm, P in zip(Ds, Ps)]
    size = SUB
    while size < c:
        off = ((row // (2 * size)) == (col // (2 * size))) & ((row // size) != (col // size))
        DCs = [_mm(Dm, jnp.where(off, L, 0.0)) for Dm, L in zip(Ds, Ls)]
        Ds = [Dm - _mm(DC, Dm) for Dm, DC in zip(Ds, DCs)]
        size *= 2
    return Ds


def _mod_body(c_ref, w_ref, b_ref, o_ref):
    o_ref[0] = _mm(_silu(c_ref[...]), w_ref[0]) + b_ref[0]


def _ada_mod(c_all, ada_w, ada_b):
    n = c_all.shape[0]
    ncol = ada_w.shape[2] // D_MODEL
    return pl.pallas_call(
        _mod_body,
        grid=(DEPTH, ncol),
        in_specs=[
            pl.BlockSpec((n, D_MODEL), lambda l, j: (0, 0)),
            pl.BlockSpec((1, D_MODEL, D_MODEL), lambda l, j: (l, 0, j)),
            pl.BlockSpec((1, 1, D_MODEL), lambda l, j: (l, 0, j)),
        ],
        out_specs=pl.BlockSpec((1, n, D_MODEL), lambda l, j: (l, 0, j)),
        out_shape=jax.ShapeDtypeStruct((DEPTH, n, ncol * D_MODEL), F32),
        compiler_params=pltpu.CompilerParams(dimension_semantics=("arbitrary", "arbitrary"),
                                             vmem_limit_bytes=VMEM_LIMIT),
        name="ada_mod",
    )(c_all, ada_w, ada_b.reshape(DEPTH, 1, -1))


def _lb_body(x_ref, o_ref):
    x = x_ref[...]
    m = jnp.max(x, axis=0, keepdims=True)
    e = jnp.exp(x - m)
    p = e / jnp.sum(e, axis=0, keepdims=True)
    acc = jnp.zeros_like(p[0:1])
    rows = [acc]
    for l in range(1, DEPTH):
        acc = acc + p[l:l + 1]
        rows.append(acc)
    o_ref[...] = jnp.concatenate(rows, axis=0)


def _lower_bounds(logits):
    return pl.pallas_call(
        _lb_body,
        out_shape=jax.ShapeDtypeStruct(logits.shape, F32),
        name="hgrn_lower_bounds",
    )(logits.astype(F32))


def _mod_spec(rm, tm, col):
    if rm == 1:
        return pl.BlockSpec((1, 1, D_MODEL), lambda b, i: (b, 0, col))
    return pl.BlockSpec((1, tm, D_MODEL), lambda b, i: (b, i, col))


def _full_spec(shape):
    nd = len(shape)
    return pl.BlockSpec(shape, lambda b, i: (0,) * nd)


def _weight_spec(shape):
    nd = len(shape)
    return pl.BlockSpec(shape, lambda b, i: (0,) * nd, pipeline_mode=pl.Buffered(1))


def _row_spec(tm, width):
    return pl.BlockSpec((1, tm, width), lambda b, i: (b, i, 0))


def _modulated_norm(x, nw, sc, sh):
    ms = jnp.mean(x * x, axis=-1, keepdims=True)
    return (x * lax.rsqrt(ms + NORM_EPS) * nw) * (1.0 + sc) + sh


def _in_body(x_ref, sc_ref, sh_ref, nw_ref, wrw_ref, wret_ref, wgdn_ref, wab_ref, whg_ref,
             h_ref, prw_ref, pret_ref, pgdn_ref, pab_ref, phg_ref):
    h = _modulated_norm(x_ref[0], nw_ref[...], sc_ref[0], sh_ref[0]).astype(BF)
    h_ref[0] = h
    prw_ref[0] = _mm(h, wrw_ref[...])
    pret_ref[0] = _mm(h, wret_ref[...])
    pgdn_ref[0] = _mm(h, wgdn_ref[...])
    pab_ref[0] = _mm(h, wab_ref[...])
    phg_ref[0] = _mm(h, whg_ref[...])


def _in_proj(x, mod, rm, tm, nw, ws):
    G, R, _ = x.shape
    wrw, wret, wgdn, wab, whg = ws
    outs = [jax.ShapeDtypeStruct((G, R, D_MODEL), BF)] + [
        jax.ShapeDtypeStruct((G, R, w.shape[1]), F32) for w in ws]
    return pl.pallas_call(
        _in_body,
        grid=(G, R // tm),
        in_specs=[_row_spec(tm, D_MODEL), _mod_spec(rm, tm, 1), _mod_spec(rm, tm, 0),
                  _full_spec((1, D_MODEL))] + [_weight_spec(w.shape) for w in ws],
        out_specs=[_row_spec(tm, D_MODEL)] + [_row_spec(tm, w.shape[1]) for w in ws],
        out_shape=outs,
        compiler_params=pltpu.CompilerParams(dimension_semantics=("arbitrary", "arbitrary"),
                                             vmem_limit_bytes=VMEM_LIMIT),
        name="in_proj",
    )(x, mod, mod, nw, wrw, wret, wgdn, wab, whg)


def _merge_ffn_body(final, x_ref, h_ref, yrw_ref, yret_ref, ygdn_ref, yhg_ref, g1_ref, sc_ref, sh_ref, g2_ref,
                    nw_ref, wg_ref, wb_ref, wo_ref, wgu_ref, wd_ref, fw_ref, o_ref):
    h = h_ref[0]
    merged = None
    for n, y_ref in enumerate((yrw_ref, yret_ref, ygdn_ref, yhg_ref)):
        gate = _sigmoid(_mm(h, wg_ref[:, n * D_MODEL:(n + 1) * D_MODEL]))
        term = gate * _mm(y_ref[0], wb_ref[n])
        merged = term if merged is None else merged + term
    x = x_ref[0] + g1_ref[0] * _mm(merged, wo_ref[...])
    h2 = _modulated_norm(x, nw_ref[...], sc_ref[0], sh_ref[0]).astype(BF)
    gate = _mm(h2, wgu_ref[:, :D_FF])
    up = _mm(h2, wgu_ref[:, D_FF:])
    x = x + g2_ref[0] * _mm(_silu(gate) * up, wd_ref[...])
    if final:
        ms = jnp.mean(x * x, axis=-1, keepdims=True)
        x = x * lax.rsqrt(ms + NORM_EPS) * fw_ref[...]
    o_ref[0] = x


def _merge_ffn(x, h, ys, mod, rm, tm, nw, wg, wb, wo, wgu, wd, fw, final):
    G, R, _ = x.shape
    return pl.pallas_call(
        functools.partial(_merge_ffn_body, final),
        grid=(G, R // tm),
        in_specs=[_row_spec(tm, D_MODEL), _row_spec(tm, D_MODEL)] + [_row_spec(tm, BR)] * 4
        + [_mod_spec(rm, tm, 2), _mod_spec(rm, tm, 4), _mod_spec(rm, tm, 3), _mod_spec(rm, tm, 5),
           _full_spec((1, D_MODEL))] + [_weight_spec(w.shape) for w in (wg, wb, wo, wgu, wd)]
        + [_full_spec((1, D_MODEL))],
        out_specs=_row_spec(tm, D_MODEL),
        out_shape=jax.ShapeDtypeStruct((G, R, D_MODEL), F32),
        compiler_params=pltpu.CompilerParams(dimension_semantics=("arbitrary", "arbitrary"),
                                             vmem_limit_bytes=VMEM_LIMIT),
        name="merge_ffn",
    )(x, h, *ys, mod, mod, mod, mod, nw, wg, wb, wo, wgu, wd, fw)


def _chunk_spec(c, width):
    return pl.BlockSpec((1, c, width), lambda b, i: (b, i, 0))


def _state_spec():
    return pl.BlockSpec((1, NH, HD, HD), lambda b, i: (b, 0, 0, 0))


def _iota2(shape):
    return (lax.broadcasted_iota(jnp.int32, shape, 0), lax.broadcasted_iota(jnp.int32, shape, 1))


def _rope(t, cosf, sins):
    lane = lax.broadcasted_iota(jnp.int32, t.shape, 1)
    width = t.shape[1]
    rot = jnp.where((lane % HD) < HD // 2, pltpu.roll(t, width - HD // 2, 1), pltpu.roll(t, HD // 2, 1))
    return t * cosf + rot * sins


def _ret_body(nc, p_ref, cos_ref, sin_ref, qin_ref, kout_ref, dmat_ref, cdec_ref, ones_ref, y_ref, s_ref):
    c = CHUNK

    @pl.when(pl.program_id(1) == 0)
    def _():
        s_ref[...] = jnp.zeros_like(s_ref)

    P = p_ref[0]
    cosf, sins = cos_ref[...], sin_ref[...]
    q = _rope(P[:, :BR], cosf, sins)
    k = _rope(P[:, BR:2 * BR], cosf, sins) * HD ** -0.5
    v = P[:, 2 * BR:3 * BR]
    g = P[:, 3 * BR:]
    qs = (q * qin_ref[...]).astype(BF)
    ko = (k * kout_ref[...]).astype(BF)
    q, k, v = q.astype(BF), k.astype(BF), v.astype(BF)
    items = [(ci, h) for ci in range(nc) for h in range(NH)]

    def sl(t, ci, h):
        return t[ci * c:(ci + 1) * c, h * HD:(h + 1) * HD]

    ss = [_mm(sl(q, ci, h), sl(k, ci, h), NT) * dmat_ref[h] for ci, h in items]
    oi = [_mm(s, sl(v, ci, h)) for s, (ci, h) in zip(ss, items)]
    kv = [_mm(sl(ko, ci, h), sl(v, ci, h), TN) for ci, h in items]
    outs = [[None] * NH for _ in range(nc)]
    for idx, (ci, h) in enumerate(items):
        S = s_ref[0, h]
        outs[ci][h] = oi[idx] + _mm(sl(qs, ci, h), S)
        s_ref[0, h] = S * cdec_ref[h] + kv[idx]
    o = jnp.concatenate([jnp.concatenate(r, axis=1) for r in outs], axis=0)
    ms = _head_sum(o * o, ones_ref[...]) * (1.0 / HD)
    y_ref[0] = (o * lax.rsqrt(ms + NORM_EPS) * _silu(g)).astype(BF)


def _ret_tables(c):
    lg = jnp.log1p(-jnp.exp2(-5.0 - jnp.arange(NH, dtype=F32)))[:, None]
    idx = jnp.arange(c, dtype=F32)
    diff = idx[:, None] - idx[None, :]
    dmat = jnp.where(diff >= 0, jnp.exp(lg[:, :, None] * jnp.maximum(diff, 0.0)), 0.0)
    q_in = jnp.exp(lg * (idx + 1.0))
    k_out = jnp.exp(lg * (c - 1.0 - idx))
    c_dec = jnp.exp(lg * c)
    expand = lambda t: jnp.repeat(t.T, HD, axis=1)
    return dmat, expand(q_in), expand(k_out), jnp.broadcast_to(c_dec[:, :, None], (NH, HD, HD))


def _rope_tables(pos):
    half = HD // 2
    freqs = ROPE_BASE ** (-jnp.arange(half, dtype=F32) / half)
    ang = pos.astype(F32)[:, None] * freqs
    cos, sin = jnp.cos(ang), jnp.sin(ang)
    cosf = jnp.tile(jnp.concatenate([cos, cos], axis=1), (1, NH))
    sins = jnp.tile(jnp.concatenate([-sin, sin], axis=1), (1, NH))
    return cosf, sins


def _ret_prompt(p, consts):
    B, T, _ = p.shape
    c = CHUNK
    nc = _chunks_per_step(T)
    tb = nc * c
    cosf, sins = _rope_tables(jnp.arange(T))
    dmat, q_in, k_out, c_dec = _ret_tables(c)
    q_in, k_out = jnp.tile(q_in, (nc, 1)), jnp.tile(k_out, (nc, 1))
    return pl.pallas_call(
        functools.partial(_ret_body, nc),
        grid=(B, T // tb),
        in_specs=[_chunk_spec(tb, D_MODEL),
                  pl.BlockSpec((tb, BR), lambda b, i: (i, 0)), pl.BlockSpec((tb, BR), lambda b, i: (i, 0)),
                  _full_spec((tb, BR)), _full_spec((tb, BR)), _full_spec((NH, c, c)), _full_spec((NH, HD, HD)),
                  _full_spec((BR, BR))],
        out_specs=[_chunk_spec(tb, BR), _state_spec()],
        out_shape=[jax.ShapeDtypeStruct((B, T, BR), BF), jax.ShapeDtypeStruct((B, NH, HD, HD), F32)],
        compiler_params=pltpu.CompilerParams(dimension_semantics=("arbitrary", "arbitrary"),
                                             vmem_limit_bytes=VMEM_LIMIT),
        name="ret_prompt",
    )(p, cosf, sins, q_in, k_out, dmat, c_dec, consts["ones_bd"])


def _chunks_per_step(T):
    for nc in (8, 4, 2):
        if T % (nc * CHUNK) == 0:
            return nc
    return 1


def _gdn_body(nc, pg_ref, pab_ref, cw_ref, alog_ref, dtb_ref, nw_ref, tri_ref, triu_ref, ones_ref,
              y_ref, s_ref, cv_ref, xext):
    c = CHUNK
    tb = nc * c

    @pl.when(pl.program_id(1) == 0)
    def _():
        s_ref[...] = jnp.zeros_like(s_ref)
        xext[pl.ds(0, 8), :] = jnp.zeros((8, CONV_CH), F32)

    P = pg_ref[0]
    X = P[:, :CONV_CH]
    xext[pl.ds(8, tb), :] = X
    cw = cw_ref[...]
    conv = xext[pl.ds(8 - 3, tb), :] * cw[0:1]
    conv = conv + xext[pl.ds(8 - 2, tb), :] * cw[1:2]
    conv = conv + xext[pl.ds(8 - 1, tb), :] * cw[2:3]
    conv = conv + X * cw[3:4]
    xext[pl.ds(0, 8), :] = X[tb - 8:tb]
    cv_ref[0] = X[tb - (GDN_CONV - 1):tb]
    conv = _silu(conv)
    ones_bd = ones_ref[...]
    qn, kn, v = conv[:, :BR], conv[:, BR:2 * BR], conv[:, 2 * BR:]
    q = qn * lax.rsqrt(_head_sum(qn * qn, ones_bd) + 1e-6) * HD ** -0.5
    k = kn * lax.rsqrt(_head_sum(kn * kn, ones_bd) + 1e-6)
    gate = P[:, CONV_CH:]

    ab = pab_ref[0]
    glog = -jnp.exp(alog_ref[...]) * _softplus(ab + dtb_ref[...])
    beta = _sigmoid(ab)
    Gc = _mm_exact_lhs(tri_ref[...], glog, NN)
    Gr = _mm_exact_rhs(glog, triu_ref[...], TN)
    eG = jnp.exp(Gc)
    row, col = _iota2((c, c))
    incl = row >= col
    strict = row > col
    items = [(ci, h) for ci in range(nc) for h in range(NH)]

    def sl(t, ci, h):
        return t[ci * c:(ci + 1) * c, h * HD:(h + 1) * HD]

    kb, qb = k.astype(BF), q.astype(BF)
    KQ = [_mm(jnp.concatenate([sl(kb, ci, h), sl(qb, ci, h)], axis=0), sl(kb, ci, h), NT) for ci, h in items]
    Ls, qks, rhss, kouts, cdecs = [], [], [], [], []
    for (ci, h), kq in zip(items, KQ):
        rows = slice(ci * c, (ci + 1) * c)
        gc = Gc[rows, h:h + 1]
        gr = Gr[h:h + 1, rows]
        bcol = beta[rows, NH + h:NH + h + 1]
        gam = jnp.where(incl, jnp.exp(jnp.where(incl, gc - gr, 0.0)), 0.0)
        Ls.append(jnp.where(strict, bcol * gam * kq[:c], 0.0))
        qks.append(kq[c:] * gam)
        rhss.append(jnp.concatenate([sl(v, ci, h) * bcol, sl(k, ci, h) * (bcol * eG[rows, h:h + 1])], axis=1))
        gend = gc[c - 1:c, :]
        kouts.append(sl(k, ci, h) * jnp.exp(gend - gc))
        cdecs.append(jnp.exp(gend))
    Ts = _tri_inverse(Ls, row, col)
    sols = [_mm(T, rhs) for T, rhs in zip(Ts, rhss)]
    QS = [_mm(qk, sol) for qk, sol in zip(qks, sols)]
    KS = [_mm(kout, sol, TN) for kout, sol in zip(kouts, sols)]
    outs = [[None] * NH for _ in range(nc)]
    for idx, (ci, h) in enumerate(items):
        rows = slice(ci * c, (ci + 1) * c)
        QW = sl(q, ci, h) * eG[rows, h:h + 1] - QS[idx][:, HD:]
        S = s_ref[0, h]
        Z = _mm(jnp.concatenate([KS[idx][:, HD:], QW], axis=0), S)
        s_ref[0, h] = S * cdecs[idx] - Z[:HD] + KS[idx][:, :HD]
        outs[ci][h] = Z[HD:] + QS[idx][:, :HD]
    o = jnp.concatenate([jnp.concatenate(r, axis=1) for r in outs], axis=0)
    ms = _head_sum(o * o, ones_bd) * (1.0 / HD)
    y_ref[0] = (o * lax.rsqrt(ms + NORM_EPS) * nw_ref[...] * _silu(gate)).astype(BF)


def _gdn_prompt(pg, pab, cw, alog, dtb, nw, consts):
    B, T, _ = pg.shape
    nc = _chunks_per_step(T)
    tb = nc * CHUNK
    return pl.pallas_call(
        functools.partial(_gdn_body, nc),
        grid=(B, T // tb),
        in_specs=[_chunk_spec(tb, D_MODEL), _chunk_spec(tb, LANES), _full_spec((GDN_CONV, CONV_CH)),
                  _full_spec((1, LANES)), _full_spec((1, LANES)), _full_spec((1, BR)),
                  _full_spec((tb, tb)), _full_spec((tb, tb)), _full_spec((BR, BR))],
        out_specs=[_chunk_spec(tb, BR), _state_spec(),
                   pl.BlockSpec((1, GDN_CONV - 1, CONV_CH), lambda b, i: (b, 0, 0))],
        out_shape=[jax.ShapeDtypeStruct((B, T, BR), BF), jax.ShapeDtypeStruct((B, NH, HD, HD), F32),
                   jax.ShapeDtypeStruct((B, GDN_CONV - 1, CONV_CH), F32)],
        scratch_shapes=[pltpu.VMEM((tb + 8, CONV_CH), F32)],
        compiler_params=pltpu.CompilerParams(dimension_semantics=("arbitrary", "arbitrary"),
                                             vmem_limit_bytes=VMEM_LIMIT),
        name="gdn_prompt",
    )(pg, pab, cw, alog, dtb, nw, consts["tri%d" % nc], consts["triu%d" % nc], consts["ones_bd"])


def _hgrn_gates(xq, xf, lb):
    q = _silu(xq)
    ls = jnp.minimum(xf, 0.0) - jnp.log1p(jnp.exp(-jnp.abs(xf)))
    pos = lb > 0.0
    a = jnp.log(jnp.where(pos, lb, 1.0))
    b = jnp.log1p(-lb) + ls
    lae = jnp.maximum(a, b) + jnp.log1p(jnp.exp(-jnp.abs(a - b)))
    log_f = jnp.where(pos, lae, ls)
    k = (1.0 - lb) * _sigmoid(-xf)
    return q, log_f, k


def _hg_body(nc, p_ref, lb_ref, nw_ref, tri_ref, ones_ref, eye_ref, y_ref, s_ref, st, tbuf, sbuf):
    c = CHUNK
    tb = nc * c
    nb = tb // SUB
    half = SUB // 2
    gi = pl.program_id(1)

    @pl.when(gi == 0)
    def _():
        st[...] = jnp.zeros_like(st)

    P = p_ref[0]
    q, log_f, k = _hgrn_gates(P[:, :BR], P[:, BR:2 * BR], lb_ref[...])
    v = P[:, 2 * BR:3 * BR]
    og = P[:, 3 * BR:]
    ones_bd = ones_ref[...]
    G = _mm_exact_lhs(tri_ref[...], log_f, NN)

    G3, q3, k3, v3 = (t.reshape(nb, SUB, BR) for t in (G, q, k, v))
    o_lo = jnp.zeros((nb, half, BR), F32)
    o_hi = jnp.zeros((nb, half, BR), F32)
    offs, total = [], 0
    for jj in range(SUB):
        offs.append(total)
        total += nb * (SUB if jj < half else half)
    for jj in range(SUB):
        i0 = 0 if jj < half else half
        rowi = lax.broadcasted_iota(jnp.int32, (nb, SUB - i0, BR), 1) + i0
        e = jnp.exp(jnp.minimum(G3[:, i0:, :] - G3[:, jj:jj + 1, :], 0.0))
        t = jnp.where(rowi >= jj, q3[:, i0:, :] * e * k3[:, jj:jj + 1, :], 0.0)
        tbuf[pl.ds(offs[jj], nb * (SUB - i0)), :] = t.reshape(nb * (SUB - i0), BR).astype(BF)
    sbuf[...] = jnp.dot(tbuf[...], ones_bd, preferred_element_type=F32)
    for jj in range(SUB):
        i0 = 0 if jj < half else half
        s = sbuf[pl.ds(offs[jj], nb * (SUB - i0)), :].reshape(nb, SUB - i0, BR)
        contrib = s * v3[:, jj:jj + 1, :]
        if i0 == 0:
            o_lo = o_lo + contrib[:, :half]
            o_hi = o_hi + contrib[:, half:]
        else:
            o_hi = o_hi + contrib
    o = jnp.concatenate([o_lo, o_hi], axis=1).reshape(tb, BR)

    pairs = [(ci, bi) for ci in range(nc) for bi in range(1, c // SUB)]
    qts, kts = [], []
    for ci, bi in pairs:
        r0 = ci * c
        ref = G[r0 + bi * SUB - 1:r0 + bi * SUB, :]
        rows_i = slice(r0 + bi * SUB, r0 + (bi + 1) * SUB)
        rows_j = slice(r0, r0 + bi * SUB)
        qts.append(q[rows_i] * jnp.exp(jnp.minimum(G[rows_i] - ref, 0.0)))
        kts.append(k[rows_j] * jnp.exp(jnp.minimum(ref - G[rows_j], 0.0)))
    wts = [[_mm(qt[:, h * HD:(h + 1) * HD], kt[:, h * HD:(h + 1) * HD], NT) for h in range(NH)]
           for qt, kt in zip(qts, kts)]
    offs = [[_mm(w[h], v[ci * c:ci * c + bi * SUB, h * HD:(h + 1) * HD]) for h in range(NH)]
            for (ci, bi), w in zip(pairs, wts)]
    off_rows = []
    idx = 0
    for ci in range(nc):
        off_rows.append(jnp.zeros((SUB, BR), F32))
        for bi in range(1, c // SUB):
            off_rows.append(jnp.concatenate(offs[idx], axis=1))
            idx += 1
    o = o + jnp.concatenate(off_rows, axis=0)

    qg = q * jnp.exp(G)
    GE = jnp.concatenate([jnp.broadcast_to(G[(ci + 1) * c - 1:(ci + 1) * c], (c, BR)) for ci in range(nc)], axis=0)
    kd = k * jnp.exp(GE - G)
    items = [(ci, h) for ci in range(nc) for h in range(NH)]

    def sl(t, ci, h):
        return t[ci * c:(ci + 1) * c, h * HD:(h + 1) * HD]

    vk = [_mm(sl(v, ci, h), sl(kd, ci, h), TN) for ci, h in items]
    outs = [[None] * NH for _ in range(nc)]
    for idx, (ci, h) in enumerate(items):
        ST = st[h]
        outs[ci][h] = _mm(sl(qg, ci, h), ST, NT)
        st[h] = ST * jnp.exp(G[(ci + 1) * c - 1:(ci + 1) * c, h * HD:(h + 1) * HD]) + vk[idx]
    o = o + jnp.concatenate([jnp.concatenate(r, axis=1) for r in outs], axis=0)
    ms = _head_sum(o * o, ones_bd) * (1.0 / HD)
    y_ref[0] = (o * lax.rsqrt(ms + NORM_EPS) * nw_ref[...] * _sigmoid(og)).astype(BF)

    @pl.when(gi == pl.num_programs(1) - 1)
    def _():
        for h in range(NH):
            s_ref[0, h] = _mm_exact_lhs(eye_ref[...], st[h], NT)


def _hg_prompt(p, lb, nw, consts):
    B, T, _ = p.shape
    nc = _chunks_per_step(T)
    tb = nc * CHUNK
    return pl.pallas_call(
        functools.partial(_hg_body, nc),
        grid=(B, T // tb),
        in_specs=[_chunk_spec(tb, D_MODEL), _full_spec((1, BR)), _full_spec((1, BR)), _full_spec((tb, tb)),
                  _full_spec((BR, BR)), _full_spec((HD, HD))],
        out_specs=[_chunk_spec(tb, BR), _state_spec()],
        out_shape=[jax.ShapeDtypeStruct((B, T, BR), BF), jax.ShapeDtypeStruct((B, NH, HD, HD), F32)],
        scratch_shapes=[pltpu.VMEM((NH, HD, HD), F32),
                        pltpu.VMEM((tb // SUB * (SUB + SUB // 2) * (SUB // 2), BR), BF),
                        pltpu.VMEM((tb // SUB * (SUB + SUB // 2) * (SUB // 2), BR), F32)],
        compiler_params=pltpu.CompilerParams(dimension_semantics=("arbitrary", "arbitrary"),
                                             vmem_limit_bytes=VMEM_LIMIT),
        name="hgrn_prompt",
    )(p, lb, nw, consts["tri%d" % nc], consts["ones_bd"], consts["eye"])


def _rw_features(P, prev, mu, w0, w2, a0, a2, g2, kkw, kaw, ones_bd):
    Pm = P + mu * (prev - P)
    r, kx, vx = Pm[:, :BR], Pm[:, BR:2 * BR], Pm[:, 2 * BR:3 * BR]
    o4 = 3 * BR + RW_LORA_W
    o5 = o4 + RW_LORA_A
    wl, al, gl = Pm[:, 3 * BR:o4], Pm[:, o4:o5], Pm[:, o5:]
    zw = w0 + _mm(jnp.tanh(wl), w2)
    lw = -jnp.exp(-_softplus(-zw) - 0.5)
    a = _sigmoid(a0 + _mm(al, a2))
    g = _mm(_sigmoid(gl), g2)
    kks = kx * kkw
    kk = kks * lax.rsqrt(_head_sum(kks * kks, ones_bd) + 1e-6)
    k = kx * (1.0 + (a - 1.0) * kaw)
    return r, lw, k, vx, -kk, kk * a, g


def _rw_output(o, r, k, vx, g, rkw, lnw, lnb, ones_bd):
    mean = _head_sum(o, ones_bd) * (1.0 / HD)
    cen = o - mean
    var = _head_sum(cen * cen, ones_bd) * (1.0 / HD)
    on = cen * lax.rsqrt(var + RW_GN_EPS) * lnw + lnb
    bonus = _head_sum(r * k * rkw, ones_bd) * vx
    return (on + bonus) * g


def _rw_body(nc, p_ref, mu_ref, w0_ref, w2_ref, a0_ref, a2_ref, g2_ref, kkw_ref, kaw_ref, rkw_ref,
             lnw_ref, lnb_ref, tri_ref, ones_ref, eye_ref, y_ref, s_ref, sh_ref, st, prev):
    c = CHUNK
    tb = nc * c
    gi = pl.program_id(1)

    @pl.when(gi == 0)
    def _():
        st[...] = jnp.zeros_like(st)
        prev[...] = jnp.zeros_like(prev)

    P = p_ref[0]
    rowp = lax.broadcasted_iota(jnp.int32, P.shape, 0)
    shifted = jnp.where(rowp == 0, prev[0:1, :], pltpu.roll(P, 1, 0))
    prev[0:1, :] = P[tb - 1:tb, :]
    sh_ref[0] = P[tb - 1:tb, :]
    ones_bd = ones_ref[...]
    r, lw, k, vx, av, bv, g = _rw_features(P, shifted, mu_ref[...], w0_ref[...], w2_ref[...], a0_ref[...],
                                           a2_ref[...], g2_ref[...], kkw_ref[...], kaw_ref[...], ones_bd)
    LG = _mm_exact_lhs(tri_ref[...], lw, NN)
    LGE = jnp.concatenate([jnp.broadcast_to(LG[(ci + 1) * c - 1:(ci + 1) * c], (c, BR)) for ci in range(nc)],
                          axis=0)
    rt = r * jnp.exp(LG)
    at = av * jnp.exp(LG - lw)
    einv = jnp.exp(-LG)
    kh_ = k * einv
    bh_ = bv * einv
    eo = jnp.exp(LGE - LG)
    kout = k * eo
    bout = bv * eo
    row, col = _iota2((c, c))
    incl = row >= col
    strict = row > col
    items = [(ci, h) for ci in range(nc) for h in range(NH)]

    def sl(t, ci, h):
        return t[ci * c:(ci + 1) * c, h * HD:(h + 1) * HD]

    atb, rtb, bhb, khb, vxb, koutb, boutb = (t.astype(BF) for t in (at, rt, bh_, kh_, vx, kout, bout))
    Ms = [_mm(jnp.concatenate([sl(atb, ci, h), sl(rtb, ci, h)], axis=0),
              jnp.concatenate([sl(bhb, ci, h), sl(khb, ci, h)], axis=0), NT) for ci, h in items]
    Aabs = [jnp.where(strict, M[:c, :c], 0.0) for M in Ms]
    Arbs = [jnp.where(incl, M[c:, :c], 0.0) for M in Ms]
    AKs = [jnp.concatenate([jnp.where(strict, M[:c, c:], 0.0), jnp.where(incl, M[c:, c:], 0.0)], axis=0)
           for M in Ms]
    AVs = [_mm(AK, sl(vxb, ci, h)) for AK, (ci, h) in zip(AKs, items)]
    Ts = _tri_inverse([-A for A in Aabs], row, col)
    WTs = [_mm(T, jnp.concatenate([AV[:c].astype(BF), sl(atb, ci, h)], axis=1))
           for T, AV, (ci, h) in zip(Ts, AVs, items)]
    AWs = [_mm(Arb, WT) for Arb, WT in zip(Arbs, WTs)]
    CBs = [_mm(WT, sl(boutb, ci, h), TN) for WT, (ci, h) in zip(WTs, items)]
    VKs = [_mm(sl(vxb, ci, h), sl(koutb, ci, h), TN) for ci, h in items]
    outs = [[None] * NH for _ in range(nc)]
    for idx, (ci, h) in enumerate(items):
        dec = jnp.exp(LG[(ci + 1) * c - 1:(ci + 1) * c, h * HD:(h + 1) * HD])
        RQ = sl(rt, ci, h) + AWs[idx][:, HD:]
        ST = st[h]
        outs[ci][h] = _mm(RQ, ST, NT) + AWs[idx][:, :HD] + AVs[idx][c:]
        st[h] = ST * dec + _mm(ST, CBs[idx][HD:]) + CBs[idx][:HD] + VKs[idx]
    o = jnp.concatenate([jnp.concatenate(rw, axis=1) for rw in outs], axis=0)
    y_ref[0] = _rw_output(o, r, k, vx, g, rkw_ref[...], lnw_ref[...], lnb_ref[...], ones_bd).astype(BF)

    @pl.when(gi == pl.num_programs(1) - 1)
    def _():
        for h in range(NH):
            s_ref[0, h] = _mm_exact_lhs(eye_ref[...], st[h], NT)


def _rw_prompt(p, lp, consts):
    B, T, _ = p.shape
    nc = _chunks_per_step(T)
    tb = nc * CHUNK
    params = [lp[n] for n in ("rw_mu", "rw_w0", "rw_w2", "rw_a0", "rw_a2", "rw_g2", "rw_k_k", "rw_k_a",
                              "rw_r_k", "rw_ln_w", "rw_ln_b")]
    return pl.pallas_call(
        functools.partial(_rw_body, nc),
        grid=(B, T // tb),
        in_specs=[_chunk_spec(tb, D_MODEL)] + [_full_spec(t.shape) for t in params]
        + [_full_spec((tb, tb)), _full_spec((BR, BR)), _full_spec((HD, HD))],
        out_specs=[_chunk_spec(tb, BR), _state_spec(),
                   pl.BlockSpec((1, 1, D_MODEL), lambda b, i: (b, 0, 0))],
        out_shape=[jax.ShapeDtypeStruct((B, T, BR), BF), jax.ShapeDtypeStruct((B, NH, HD, HD), F32),
                   jax.ShapeDtypeStruct((B, 1, D_MODEL), F32)],
        scratch_shapes=[pltpu.VMEM((NH, HD, HD), F32), pltpu.VMEM((8, D_MODEL), F32)],
        compiler_params=pltpu.CompilerParams(dimension_semantics=("arbitrary", "arbitrary"),
                                             vmem_limit_bytes=VMEM_LIMIT),
        name="rwkv_prompt",
    )(p, *params, consts["tri%d" % nc], consts["ones_bd"], consts["eye"])


_VT = ("rw_a", "rw_w", "rw_b", "rw_k", "rw_r", "rw_v", "ret_q", "ret_k", "ret_v", "gdn_q", "gdn_k", "gdn_v",
       "hg_f", "hg_k", "hg_q", "hg_v")
_VT_EG, _VT_BETA = len(_VT), len(_VT) + 1


def _stepT_body(prw_ref, pret_ref, pgdn_ref, pab_ref, phg_ref,
                srw_ref, ssh_ref, sret_ref, sgdn_ref, scv_ref, shg_ref,
                mu_ref, w0_ref, w2_ref, a0_ref, a2_ref, g2_ref, kkw_ref, kaw_ref, rkw_ref, lnw_ref, lnb_ref,
                cos_ref, sin_ref, gam_ref, cw_ref, alog_ref, dtb_ref, gnw_ref, lb_ref, hnw_ref, ones_ref,
                yrw_ref, yret_ref, ygdn_ref, yhg_ref,
                nrw_ref, nsh_ref, nret_ref, ngdn_ref, ncv_ref, nhg_ref, vt, ot):
    h = pl.program_id(0)
    ones_bd = ones_ref[...]

    def features():
        f = {}
        r, lw, k, vx, av, bv, g = _rw_features(prw_ref[...], ssh_ref[...], mu_ref[...], w0_ref[...], w2_ref[...],
                                               a0_ref[...], a2_ref[...], g2_ref[...], kkw_ref[...], kaw_ref[...],
                                               ones_bd)
        f.update(rw_a=av, rw_w=jnp.exp(lw), rw_b=bv, rw_k=k, rw_r=r, rw_v=vx, rw_g=g)
        P = pret_ref[...]
        cosf, sins = cos_ref[...], sin_ref[...]
        f.update(ret_q=_rope(P[:, :BR], cosf, sins), ret_k=_rope(P[:, BR:2 * BR], cosf, sins) * HD ** -0.5,
                 ret_v=P[:, 2 * BR:3 * BR])
        P = pgdn_ref[...]
        cw = cw_ref[...]
        conv = (scv_ref[:, 0, :] * cw[0:1] + scv_ref[:, 1, :] * cw[1:2] + scv_ref[:, 2, :] * cw[2:3]
                + P[:, :CONV_CH] * cw[3:4])
        conv = _silu(conv)
        qn, kn = conv[:, :BR], conv[:, BR:2 * BR]
        ab = pab_ref[...]
        f.update(gdn_q=qn * lax.rsqrt(_head_sum(qn * qn, ones_bd) + 1e-6) * HD ** -0.5,
                 gdn_k=kn * lax.rsqrt(_head_sum(kn * kn, ones_bd) + 1e-6), gdn_v=conv[:, 2 * BR:],
                 eg=jnp.exp(-jnp.exp(alog_ref[...]) * _softplus(ab + dtb_ref[...])), beta=_sigmoid(ab))
        P = phg_ref[...]
        q, log_f, k = _hgrn_gates(P[:, :BR], P[:, BR:2 * BR], lb_ref[...])
        f.update(hg_f=jnp.exp(log_f), hg_k=k, hg_q=q, hg_v=P[:, 2 * BR:3 * BR])
        return f

    @pl.when(h == 0)
    def _():
        f = features()
        nsh_ref[...] = prw_ref[...]
        ncv_ref[:, 0, :] = scv_ref[:, 1, :]
        ncv_ref[:, 1, :] = scv_ref[:, 2, :]
        ncv_ref[:, 2, :] = pgdn_ref[:, :CONV_CH]
        for i, name in enumerate(_VT):
            vt[i] = f[name].T
        vt[_VT_EG, 0:LANES] = f["eg"].T
        vt[_VT_BETA, 0:LANES] = f["beta"].T

    r0 = pl.multiple_of(h * HD, HD)

    def vec(name):
        return vt[_VT.index(name), pl.ds(r0, HD), :]

    def kcol(name):
        return vec(name)[:, None, :]

    def vrow(name):
        return vec(name)[None, :, :]

    S = srw_ref[...]
    sa = jnp.sum(kcol("rw_a") * S, axis=0)
    S = kcol("rw_w") * S + kcol("rw_b") * sa[None] + kcol("rw_k") * vrow("rw_v")
    nrw_ref[...] = S
    ot[0, pl.ds(r0, HD), :] = jnp.sum(kcol("rw_r") * S, axis=0)

    S = sret_ref[...] * gam_ref[pl.ds(h, 1), :] + kcol("ret_k") * vrow("ret_v")
    nret_ref[...] = S
    ot[1, pl.ds(r0, HD), :] = jnp.sum(kcol("ret_q") * S, axis=0)

    eg = vt[_VT_EG, pl.ds(h, 1), :]
    bh = vt[_VT_BETA, pl.ds(NH + h, 1), :]
    S = sgdn_ref[...]
    kS = jnp.sum(kcol("gdn_k") * S, axis=0)
    u = bh * (vec("gdn_v") - eg * kS)
    S = eg * S + kcol("gdn_k") * u[None]
    ngdn_ref[...] = S
    ot[2, pl.ds(r0, HD), :] = jnp.sum(kcol("gdn_q") * S, axis=0)

    S = shg_ref[...] * kcol("hg_f") + kcol("hg_k") * vrow("hg_v")
    nhg_ref[...] = S
    ot[3, pl.ds(r0, HD), :] = jnp.sum(kcol("hg_q") * S, axis=0)

    @pl.when(h == NH - 1)
    def _():
        f = features()
        yrw_ref[...] = _rw_output(ot[0].T, f["rw_r"], f["rw_k"], f["rw_v"], f["rw_g"], rkw_ref[...], lnw_ref[...],
                                  lnb_ref[...], ones_bd).astype(BF)
        o = ot[1].T
        ms = _head_sum(o * o, ones_bd) * (1.0 / HD)
        yret_ref[...] = (o * lax.rsqrt(ms + NORM_EPS) * _silu(pret_ref[:, 3 * BR:])).astype(BF)
        o = ot[2].T
        ms = _head_sum(o * o, ones_bd) * (1.0 / HD)
        ygdn_ref[...] = (o * lax.rsqrt(ms + NORM_EPS) * gnw_ref[...] * _silu(pgdn_ref[:, CONV_CH:])).astype(BF)
        o = ot[3].T
        ms = _head_sum(o * o, ones_bd) * (1.0 / HD)
        yhg_ref[...] = (o * lax.rsqrt(ms + NORM_EPS) * hnw_ref[...] * _sigmoid(phg_ref[:, 3 * BR:])).astype(BF)


def _stepT_sample(ps, states_t, shift, conv, layer, lp, lb, consts):
    prw, pret, pgdn, pab, phg = ps
    n = prw.shape[0]
    cosf, sins = _rope_tables(PAST_LEN + jnp.arange(1))
    gam = jnp.broadcast_to((1.0 - jnp.exp2(-5.0 - jnp.arange(8, dtype=F32)))[:, None], (8, n))
    params = [lp[nm] for nm in ("rw_mu", "rw_w0", "rw_w2", "rw_a0", "rw_a2", "rw_g2", "rw_k_k", "rw_k_a",
                                "rw_r_k", "rw_ln_w", "rw_ln_b")]
    params += [cosf, sins, gam, lp["gdn_conv_w"], lp["gdn_a_log"], lp["gdn_dt_bias"], lp["gdn_norm_w"], lb,
               lp["hg_norm_w"], consts["ones_bd"]]
    full = lambda shape: pl.BlockSpec(shape, lambda h: (0,) * len(shape))
    st_in = pl.BlockSpec((None, None, HD, HD, n), lambda h: (layer, h, 0, 0, 0))
    st_out = pl.BlockSpec((None, HD, HD, n), lambda h: (h, 0, 0, 0))
    sh_in = pl.BlockSpec((None, n, None, D_MODEL), lambda h: (layer, 0, 0, 0))
    cv_in = pl.BlockSpec((None, n, GDN_CONV - 1, CONV_CH), lambda h: (layer, 0, 0, 0))
    st_shape = jax.ShapeDtypeStruct((NH, HD, HD, n), F32)
    return pl.pallas_call(
        _stepT_body,
        grid=(NH,),
        in_specs=[full((n, D_MODEL)), full((n, D_MODEL)), full((n, D_MODEL)), full((n, LANES)), full((n, D_MODEL)),
                  st_in, sh_in, st_in, st_in, cv_in, st_in] + [full(t.shape) for t in params],
        out_specs=[full((n, BR))] * 4 + [st_out, full((n, D_MODEL)), st_out, st_out,
                                         full((n, GDN_CONV - 1, CONV_CH)), st_out],
        out_shape=[jax.ShapeDtypeStruct((n, BR), BF)] * 4
        + [st_shape, jax.ShapeDtypeStruct((n, D_MODEL), F32), st_shape, st_shape,
           jax.ShapeDtypeStruct((n, GDN_CONV - 1, CONV_CH), F32), st_shape],
        scratch_shapes=[pltpu.VMEM((len(_VT) + 2, BR, n), F32), pltpu.VMEM((4, BR, n), F32)],
        compiler_params=pltpu.CompilerParams(dimension_semantics=("arbitrary",),
                                             vmem_limit_bytes=VMEM_LIMIT),
        name="sample_step",
    )(prw, pret, pgdn, pab, phg, states_t[0], shift, states_t[1], states_t[2], conv, states_t[3], *params)


def _constants():
    d = jnp.arange(BR)
    ones_bd = ((d[:, None] // HD) == (d[None, :] // HD)).astype(BF)
    consts = {"ones_bd": ones_bd, "eye": jnp.eye(HD, dtype=BF)}
    for nc in (1, 2, 4, 8):
        i = jnp.arange(nc * CHUNK)
        tri = ((i[:, None] >= i[None, :]) & ((i[:, None] // CHUNK) == (i[None, :] // CHUNK))).astype(BF)
        consts["tri%d" % nc] = tri
        consts["triu%d" % nc] = tri.T
    consts["tri"] = consts["tri1"]
    return consts


def _layer_params(l, a):
    row = lambda t: t[l].reshape(1, -1).astype(F32)
    o_ret, o_gdn = D_MODEL, 2 * D_MODEL
    o_ab = o_gdn + 4 * BR
    o_hg = o_ab + 2 * NH
    o_gate = o_hg + 4 * BR
    w_in = a["w_in"][l]
    wab = jnp.pad(w_in[:, o_ab:o_hg], ((0, 0), (0, LANES - 2 * NH)))
    pad_row = lambda t: jnp.pad(t[l].reshape(1, -1).astype(F32), ((0, 0), (0, LANES - NH)))
    return {
        "norm_mix_w": row(a["norm_mix_w"]), "norm_ffn_w": row(a["norm_ffn_w"]),
        "w_mix": tuple(w.astype(BF) for w in (w_in[:, :o_ret], w_in[:, o_ret:o_gdn], w_in[:, o_gdn:o_ab], wab,
                                               w_in[:, o_hg:o_gate])),
        "w_gate": w_in[:, o_gate:].astype(BF), "w_branch": a["w_branch"][l].astype(BF),
        "w_out": a["w_out"][l].astype(BF), "w_gate_up": a["w_gate_up"][l].astype(BF),
        "w_down": a["w_down"][l].astype(BF),
        "rw_mu": row(a["rw_mu"]), "rw_w0": row(a["rw_w0"]), "rw_w2": a["rw_w2"][l].astype(BF),
        "rw_a0": row(a["rw_a0"]), "rw_a2": a["rw_a2"][l].astype(BF), "rw_g2": a["rw_g2"][l].astype(BF),
        "rw_k_k": row(a["rw_k_k"]), "rw_k_a": row(a["rw_k_a"]), "rw_r_k": row(a["rw_r_k"]),
        "rw_ln_w": row(a["rw_ln_w"]), "rw_ln_b": row(a["rw_ln_b"]),
        "gdn_conv_w": a["gdn_conv_w"][l].astype(F32), "gdn_a_log": pad_row(a["gdn_a_log"]),
        "gdn_dt_bias": pad_row(a["gdn_dt_bias"]),
        "gdn_norm_w": jnp.tile(a["gdn_norm_w"][l].reshape(1, HD).astype(F32), (1, NH)),
        "hg_norm_w": row(a["hg_norm_w"]),
    }


def kernel(x_prompt, x_sample, c_prompt, c_sample, state_rwkv, state_rwkv_shift, state_ret, state_gdn, state_gdn_conv, state_hgrn, ada_w, ada_b, norm_mix_w, w_in, rw_mu, rw_w0, rw_w2, rw_a0, rw_a2, rw_g2, rw_k_k, rw_k_a, rw_r_k, rw_ln_w, rw_ln_b, gdn_conv_w, gdn_a_log, gdn_dt_bias, gdn_norm_w, hg_lb_logits, hg_norm_w, w_branch, w_out, norm_ffn_w, w_gate_up, w_down, final_norm_w):
    a = dict(norm_mix_w=norm_mix_w, w_in=w_in, rw_mu=rw_mu, rw_w0=rw_w0, rw_w2=rw_w2, rw_a0=rw_a0, rw_a2=rw_a2,
             rw_g2=rw_g2, rw_k_k=rw_k_k, rw_k_a=rw_k_a, rw_r_k=rw_r_k, rw_ln_w=rw_ln_w, rw_ln_b=rw_ln_b,
             gdn_conv_w=gdn_conv_w, gdn_a_log=gdn_a_log, gdn_dt_bias=gdn_dt_bias, gdn_norm_w=gdn_norm_w,
             hg_norm_w=hg_norm_w, w_branch=w_branch, w_out=w_out, norm_ffn_w=norm_ffn_w, w_gate_up=w_gate_up,
             w_down=w_down)
    depth = ada_w.shape[0]
    Bp, Tp, _ = x_prompt.shape
    Bs = x_sample.shape[0]
    assert x_sample.shape[1] == 1 and Tp % CHUNK == 0
    tm = next((t for t in (512, 256) if Tp % t == 0), CHUNK)
    tm2 = min(tm, 256)
    consts = _constants()
    mod = _ada_mod(jnp.concatenate([c_prompt, c_sample], axis=0), ada_w.astype(F32), ada_b.astype(F32))
    lbs = _lower_bounds(hg_lb_logits)
    fw = final_norm_w.reshape(1, D_MODEL).astype(F32)

    xp = x_prompt
    xs = x_sample.reshape(1, Bs, D_MODEL)
    states_t = tuple(jnp.transpose(s, (0, 2, 3, 4, 1)) for s in (state_rwkv, state_ret, state_gdn, state_hgrn))
    new_p = [[] for _ in range(6)]
    new_s = [[] for _ in range(6)]
    for l in range(depth):
        lp = _layer_params(l, a)
        lb = lbs[l:l + 1]
        mod_p = mod[l, :Bp].reshape(Bp, 1, -1)
        mod_s = mod[l, Bp:].reshape(1, Bs, -1)
        final = l == depth - 1

        h, prw, pret, pgdn, pab, phg = _in_proj(xp, mod_p, 1, tm, lp["norm_mix_w"], lp["w_mix"])
        y_rw, s_rw, s_sh = _rw_prompt(prw, lp, consts)
        y_ret, s_ret = _ret_prompt(pret, consts)
        y_gdn, s_gdn, s_cv = _gdn_prompt(pgdn, pab, lp["gdn_conv_w"], lp["gdn_a_log"], lp["gdn_dt_bias"],
                                         lp["gdn_norm_w"], consts)
        y_hg, s_hg = _hg_prompt(phg, lb, lp["hg_norm_w"], consts)
        xp = _merge_ffn(xp, h, (y_rw, y_ret, y_gdn, y_hg), mod_p, 1, tm2, lp["norm_ffn_w"], lp["w_gate"],
                        lp["w_branch"], lp["w_out"], lp["w_gate_up"], lp["w_down"], fw, final)
        for lst, s in zip(new_p, (s_rw, s_sh, s_ret, s_gdn, s_cv, s_hg)):
            lst.append(s)

        h, prw, pret, pgdn, pab, phg = _in_proj(xs, mod_s, Bs, Bs, lp["norm_mix_w"], lp["w_mix"])
        outs = _stepT_sample(tuple(t[0] for t in (prw, pret, pgdn, pab, phg)), states_t, state_rwkv_shift,
                             state_gdn_conv, l, lp, lb, consts)
        ys = tuple(t[None] for t in outs[:4])
        xs = _merge_ffn(xs, h, ys, mod_s, Bs, Bs, lp["norm_ffn_w"], lp["w_gate"], lp["w_branch"], lp["w_out"],
                        lp["w_gate_up"], lp["w_down"], fw, final)
        n_rw, n_sh, n_ret, n_gdn, n_cv, n_hg = outs[4:]
        for lst, s in zip(new_s, (n_rw, n_sh.reshape(Bs, 1, -1), n_ret, n_gdn, n_cv, n_hg)):
            lst.append(s)

    y_prompt = xp
    y_sample = xs.reshape(Bs, 1, D_MODEL)
    s_out = [jnp.stack(t) for t in new_s]
    for i in (0, 2, 3, 5):
        s_out[i] = jnp.transpose(s_out[i], (0, 4, 1, 2, 3))
    return (y_prompt, y_sample) + tuple(jnp.stack(t) for t in new_p) + tuple(s_out)
```
